```python
import jax, jax.numpy as jnp
from jax import lax
import numpy as np

D_MODEL = 2048
BATCH = 4
SEQ = 8192
DEPTH = 2

HEAD_DIM = 128
FOX_HEADS = 8
CONV_GROUPS = 8
RET_HEADS = 8
FOX_W = FOX_HEADS * HEAD_DIM
CONV_W = CONV_GROUPS * HEAD_DIM
RET_W = RET_HEADS * HEAD_DIM
N_BRANCH = 3
CONV_K = 3
BLOCK = 128
MEM_LEN = 256
CROSS_HEADS = 4
CROSS_W = CROSS_HEADS * HEAD_DIM
D_FF = ((8 * D_MODEL // 3 + 255) // 256) * 256
ROPE_BASE = 10000.0
EPS = 1e-6
N_IN = 3 * FOX_W + FOX_HEADS + 3 * CONV_W + 4 * RET_W + N_BRANCH * D_MODEL

kernel_name = "hybrid_fox_shortconv_retention_gated_merge"


def rmsnorm(x, g):
    xf = x.astype(jnp.float32)
    y = xf * lax.rsqrt(jnp.mean(xf * xf, axis=-1, keepdims=True) + EPS)
    return (y * g.astype(jnp.float32)).astype(x.dtype)


def causal_dwconv(u, w):
    k_len = w.shape[0]
    s = u.shape[1]
    up = jnp.pad(u, ((0, 0), (k_len - 1, 0), (0, 0)))
    return sum(up[:, k:k + s] * w[k] for k in range(k_len))


def split_heads(t, n):
    b, s, _ = t.shape
    return t.reshape(b, s, n, -1).transpose(0, 2, 1, 3)


def merge_heads(t):
    b, n, s, d = t.shape
    return t.transpose(0, 2, 1, 3).reshape(b, s, n * d)


def rotary(t, pos):
    half = t.shape[-1] // 2
    inv = ROPE_BASE ** (-jnp.arange(half, dtype=jnp.float32) / half)
    ang = pos.astype(jnp.float32)[:, None] * inv[None, :]
    cos, sin = jnp.cos(ang), jnp.sin(ang)
    t1 = t[..., :half].astype(jnp.float32)
    t2 = t[..., half:].astype(jnp.float32)
    return jnp.concatenate([t1 * cos - t2 * sin, t1 * sin + t2 * cos], axis=-1).astype(t.dtype)


def forgetting_attention(q, k, v, f_logit):
    s_len = q.shape[2]
    scale = q.shape[-1] ** -0.5
    c = jnp.cumsum(jax.nn.log_sigmoid(f_logit.astype(jnp.float32)), axis=-1)
    outs = []
    for i in range(s_len // BLOCK):
        q0, q1 = i * BLOCK, (i + 1) * BLOCK
        sc = jnp.einsum('bhqd,bhkd->bhqk', q[:, :, q0:q1], k[:, :, :q1],
                        preferred_element_type=jnp.float32) * scale
        sc = sc + c[:, :, q0:q1, None] - c[:, :, None, :q1]
        causal = jnp.arange(q1)[None, :] <= jnp.arange(q0, q1)[:, None]
        sc = jnp.where(causal, sc, -jnp.inf)
        p = jax.nn.softmax(sc, axis=-1).astype(v.dtype)
        outs.append(jnp.einsum('bhqk,bhkd->bhqd', p, v[:, :, :q1]))
    return jnp.concatenate(outs, axis=2)


def retention(q, k, v, gammas):
    b, h, s_len, dk = q.shape
    dv = v.shape[-1]
    n_chunk = s_len // BLOCK
    qf = q.astype(jnp.float32).reshape(b, h, n_chunk, BLOCK, dk) * (dk ** -0.5)
    kf = k.astype(jnp.float32).reshape(b, h, n_chunk, BLOCK, dk)
    vf = v.astype(jnp.float32).reshape(b, h, n_chunk, BLOCK, dv)
    log_g = jnp.log(gammas)
    idx = jnp.arange(BLOCK, dtype=jnp.float32)
    diff = idx[:, None] - idx[None, :]
    dmask = jnp.where(diff >= 0, jnp.exp(log_g[:, None, None] * jnp.maximum(diff, 0.0)), 0.0)
    att = jnp.einsum('bhnid,bhnjd->bhnij', qf, kf) * dmask[None, :, None]
    intra = jnp.einsum('bhnij,bhnje->bhnie', att, vf)
    k_dec = kf * jnp.exp(log_g[:, None] * (BLOCK - 1 - idx))[None, :, None, :, None]
    contrib = jnp.einsum('bhncd,bhnce->bhnde', k_dec, vf)
    chunk_decay = jnp.exp(log_g * BLOCK)[None, :, None, None]

    def step(state, u):
        return chunk_decay * state + u, state

    _, s_prev = lax.scan(step, jnp.zeros((b, h, dk, dv), jnp.float32), jnp.moveaxis(contrib, 2, 0))
    s_prev = jnp.moveaxis(s_prev, 0, 2)
    q_dec = qf * jnp.exp(log_g[:, None] * (idx + 1.0))[None, :, None, :, None]
    cross = jnp.einsum('bhncd,bhnde->bhnce', q_dec, s_prev)
    return (intra + cross).reshape(b, h, s_len, dv)


def head_groupnorm(y, g, b):
    mu = jnp.mean(y, axis=-1, keepdims=True)
    var = jnp.mean(jnp.square(y - mu), axis=-1, keepdims=True)
    yn = merge_heads((y - mu) * lax.rsqrt(var + EPS))
    return yn * g.astype(jnp.float32) + b.astype(jnp.float32)


def memory_cross_attention(h, mem_n, w_cq, w_ckv, w_co):
    q = split_heads(h @ w_cq, CROSS_HEADS)
    k, v = jnp.split(mem_n @ w_ckv, 2, axis=-1)
    k = split_heads(k, CROSS_HEADS)
    v = split_heads(v, CROSS_HEADS)
    sc = jnp.einsum('bhqd,bhkd->bhqk', q, k, preferred_element_type=jnp.float32) * (HEAD_DIM ** -0.5)
    p = jax.nn.softmax(sc, axis=-1).astype(v.dtype)
    return merge_heads(jnp.einsum('bhqk,bhkd->bhqd', p, v)) @ w_co


def conv_glu_ffn(h, w_up, conv_w, conv_b, w_down):
    u = causal_dwconv(h @ w_up, conv_w) + conv_b
    a, g = jnp.split(u, 2, axis=-1)
    return (jax.nn.silu(g) * a) @ w_down


def setup_inputs(seed: int = 0) -> dict:
    key = jax.random.key(seed)
    ks = jax.random.split(key, 26)
    L = DEPTH

    def nrm(k, shape, fan_in):
        return jax.random.normal(k, shape, jnp.float32) * (fan_in ** -0.5)

    def gain(k, shape):
        return 1.0 + 0.02 * jax.random.normal(k, shape, jnp.float32)

    def small(k, shape):
        return 0.01 * jax.random.normal(k, shape, jnp.float32)

    b_f = (jnp.linspace(1.0, 5.0, FOX_HEADS, dtype=jnp.float32)[None, :]
           + 0.1 * jax.random.normal(ks[3], (L, FOX_HEADS), jnp.float32))
    return {
        "x": jax.random.normal(ks[0], (BATCH, SEQ, D_MODEL), jnp.float32),
        "mem": jax.random.normal(ks[1], (BATCH, MEM_LEN, D_MODEL), jnp.float32),
        "g_mix": gain(ks[2], (L, D_MODEL)),
        "w_in": nrm(ks[4], (L, D_MODEL, N_IN), D_MODEL),
        "b_f": b_f,
        "b_gate": small(ks[5], (L, N_BRANCH * D_MODEL)),
        "conv_w": nrm(ks[6], (L, CONV_K, CONV_W), CONV_K),
        "ret_gn_g": gain(ks[7], (L, RET_W)),
        "ret_gn_b": small(ks[8], (L, RET_W)),
        "w_fox_o": nrm(ks[9], (L, FOX_W, D_MODEL), FOX_W),
        "w_conv_o": nrm(ks[10], (L, CONV_W, D_MODEL), CONV_W),
        "w_ret_o": nrm(ks[11], (L, RET_W, D_MODEL), RET_W),
        "w_out": nrm(ks[12], (L, D_MODEL, D_MODEL), D_MODEL),
        "g_cross": gain(ks[13], (L, D_MODEL)),
        "g_mem": gain(ks[14], (L, D_MODEL)),
        "w_cq": nrm(ks[15], (L, D_MODEL, CROSS_W), D_MODEL),
        "w_ckv": nrm(ks[16], (L, D_MODEL, 2 * CROSS_W), D_MODEL),
        "w_co": nrm(ks[17], (L, CROSS_W, D_MODEL), CROSS_W),
        "g_ffn": gain(ks[18], (L, D_MODEL)),
        "w_up": nrm(ks[19], (L, D_MODEL, 2 * D_FF), D_MODEL),
        "ffn_conv_w": nrm(ks[20], (L, CONV_K, 2 * D_FF), CONV_K),
        "ffn_conv_b": small(ks[21], (L, 2 * D_FF)),
        "w_down": nrm(ks[22], (L, D_FF, D_MODEL), D_FF),
        "g_final": gain(ks[23], (D_MODEL,)),
    }


def reference(x, mem, g_mix, w_in, b_f, b_gate, conv_w, ret_gn_g, ret_gn_b, w_fox_o, w_conv_o,
              w_ret_o, w_out, g_cross, g_mem, w_cq, w_ckv, w_co, g_ffn, w_up, ffn_conv_w,
              ffn_conv_b, w_down, g_final):
    s_len = x.shape[1]
    pos = jnp.arange(s_len)
    gammas = 1.0 - 2.0 ** (-5.0 - jnp.arange(RET_HEADS, dtype=jnp.float32))
    sizes = [FOX_W] * 3 + [FOX_HEADS] + [CONV_W] * 3 + [RET_W] * 4 + [D_MODEL] * N_BRANCH
    cuts = [int(c) for c in np.cumsum(sizes)[:-1]]
    for l in range(DEPTH):
        h = rmsnorm(x, g_mix[l])
        z = h @ w_in[l]
        (fq, fk, fv, fl, cb, cc, ch, rq, rk, rv, rg, ga, gb, gc) = jnp.split(z, cuts, axis=-1)

        f_logit = (fl + b_f[l]).transpose(0, 2, 1)
        y_a = merge_heads(forgetting_attention(split_heads(fq, FOX_HEADS), split_heads(fk, FOX_HEADS),
                                               split_heads(fv, FOX_HEADS), f_logit)) @ w_fox_o[l]

        y_b = (cb * causal_dwconv(cc * ch, conv_w[l])) @ w_conv_o[l]

        r = retention(rotary(split_heads(rq, RET_HEADS), pos), rotary(split_heads(rk, RET_HEADS), pos),
                      split_heads(rv, RET_HEADS), gammas)
        r = head_groupnorm(r, ret_gn_g[l], ret_gn_b[l]).astype(x.dtype)
        y_c = (jax.nn.silu(rg) * r) @ w_ret_o[l]

        gates = jax.nn.sigmoid(jnp.concatenate([ga, gb, gc], axis=-1) + b_gate[l])
        g_a, g_b, g_c = jnp.split(gates, N_BRANCH, axis=-1)
        x = x + (g_a * y_a + g_b * y_b + g_c * y_c) @ w_out[l]

        x = x + memory_cross_attention(rmsnorm(x, g_cross[l]), rmsnorm(mem, g_mem[l]),
                                       w_cq[l], w_ckv[l], w_co[l])

        x = x + conv_glu_ffn(rmsnorm(x, g_ffn[l]), w_up[l], ffn_conv_w[l], ffn_conv_b[l], w_down[l])
    return rmsnorm(x, g_final)
```

```python
import functools

import numpy as np
import jax
import jax.numpy as jnp
from jax import lax
from jax.experimental import pallas as pl
from jax.experimental.pallas import tpu as pltpu

F32 = jnp.float32
BF16 = jnp.bfloat16

HEAD_DIM = 128
N_HEADS = 8
CROSS_HEADS = 4
ROPE_BASE = 10000.0
EPS = 1e-6
NEG = -1e30

LANES = 128
BF16_ROWS = 16
VMEM_LIMIT = 56 * 1024 * 1024

TM = 1024
TN = 512
TM_CROSS = 512
TQ = 512
TK = 512
CUM_CHUNK = 256
RET_CHUNK = 256
RET_ROWS = 1024

Z_FQ, Z_FK, Z_FV, Z_CB, Z_CC, Z_CH, Z_RQ, Z_RK, Z_RV, Z_RG = range(10)
Z_GATES = 10
Z_WIDTH = 16 * 1024


def _cparams(*sem):
    return pltpu.CompilerParams(dimension_semantics=sem, vmem_limit_bytes=VMEM_LIMIT)


def _rms(x, g):
    ms = jnp.mean(x * x, axis=-1, keepdims=True)
    return x * lax.rsqrt(ms + EPS) * g


def _sigmoid(x):
    return 1.0 / (1.0 + jnp.exp(-x))


def _dot(a, b):
    return jnp.dot(a, b, preferred_element_type=F32)


def _dot_nt(a, b):
    return lax.dot_general(a, b, (((1,), (1,)), ((), ())), preferred_element_type=F32)


def _norm_proj_kernel(x_ref, g_ref, w_ref, *rest, with_fl):
    if with_fl:
        wfl_ref, z_ref, fl_ref, h_scr = rest
    else:
        z_ref, h_scr = rest

    @pl.when(pl.program_id(1) == 0)
    def _():
        h = _rms(x_ref[...], g_ref[...]).astype(BF16)
        h_scr[...] = h
        if with_fl:
            fl_ref[...] = _dot(h, wfl_ref[...])

    z_ref[...] = _dot(h_scr[...], w_ref[...]).astype(z_ref.dtype)


def _norm_proj(x, g, w, wfl=None, *, tm, tn):
    m, d = x.shape
    n = w.shape[1]
    with_fl = wfl is not None
    in_specs = [
        pl.BlockSpec((tm, d), lambda i, j: (i, 0)),
        pl.BlockSpec((1, d), lambda i, j: (0, 0)),
        pl.BlockSpec((d, tn), lambda i, j: (0, j)),
    ]
    out_shape = [jax.ShapeDtypeStruct((m, n), BF16)]
    out_specs = [pl.BlockSpec((tm, tn), lambda i, j: (i, j))]
    args = [x, g.reshape(1, d), w]
    if with_fl:
        in_specs.append(pl.BlockSpec((d, LANES), lambda i, j: (0, 0)))
        out_shape.append(jax.ShapeDtypeStruct((m, LANES), F32))
        out_specs.append(pl.BlockSpec((tm, LANES), lambda i, j: (i, 0)))
        args.append(wfl)
    res = pl.pallas_call(
        functools.partial(_norm_proj_kernel, with_fl=with_fl),
        out_shape=out_shape,
        grid=(m // tm, n // tn),
        in_specs=in_specs,
        out_specs=out_specs,
        scratch_shapes=[pltpu.VMEM((tm, d), BF16)],
        compiler_params=_cparams("parallel", "arbitrary"),
        name="norm_proj_fl" if with_fl else "norm_proj",
    )(*args)
    return res if with_fl else res[0]


def _forget_cumsum_kernel(fl_ref, bf_ref, ccol_ref, crow_ref, *, chunk):
    n = fl_ref.shape[0] // chunk
    r_i = lax.broadcasted_iota(jnp.int32, (chunk, chunk), 0)
    c_i = lax.broadcasted_iota(jnp.int32, (chunk, chunk), 1)
    tri = (r_i >= c_i).astype(BF16)

    def body(c, carry):
        r = pl.multiple_of(c * chunk, chunk)
        t = fl_ref[pl.ds(r, chunk), :] + bf_ref[...]
        ls = jnp.minimum(t, 0.0) - jnp.log1p(jnp.exp(-jnp.abs(t)))
        hi = ls.astype(BF16)
        r1 = ls - hi.astype(F32)
        mid = r1.astype(BF16)
        lo = (r1 - mid.astype(F32)).astype(BF16)
        cs = _dot(tri, hi) + _dot(tri, mid) + _dot(tri, lo) + carry
        ccol_ref[pl.ds(r, chunk), :] = cs
        crow_ref[:, pl.ds(r, chunk)] = cs.T
        return cs[chunk - 1:chunk, :]

    lax.fori_loop(0, n, body, jnp.zeros((1, LANES), F32))


def _forget_cumsum(fl, bf_pad, *, batch, seq):
    m = fl.shape[0]
    return pl.pallas_call(
        functools.partial(_forget_cumsum_kernel, chunk=CUM_CHUNK),
        out_shape=[jax.ShapeDtypeStruct((m, LANES), F32),
                   jax.ShapeDtypeStruct((batch * LANES, seq), F32)],
        grid=(batch,),
        in_specs=[pl.BlockSpec((seq, LANES), lambda b: (b, 0)),
                  pl.BlockSpec((1, LANES), lambda b: (0, 0))],
        out_specs=[pl.BlockSpec((seq, LANES), lambda b: (b, 0)),
                   pl.BlockSpec((LANES, seq), lambda b: (b, 0))],
        compiler_params=_cparams("parallel"),
        name="forget_cumsum",
    )(fl, bf_pad)


def _fox_kernel(q_ref, k_ref, v_ref, ccol_ref, crow_ref, o_ref, m_scr, l_scr, acc_scr,
                *, tq, tk, scale):
    h = pl.program_id(1)
    i = pl.program_id(2)
    q = q_ref[...]
    lane = lax.broadcasted_iota(jnp.int32, (tq, LANES), 1)
    cq = jnp.sum(jnp.where(lane == h, ccol_ref[...], 0.0), axis=-1, keepdims=True)

    m_scr[...] = jnp.full(m_scr.shape, NEG, F32)
    l_scr[...] = jnp.zeros(l_scr.shape, F32)
    acc_scr[...] = jnp.zeros(acc_scr.shape, F32)

    def step(j, masked):
        r = pl.multiple_of(j * tk, tk)
        k = k_ref[pl.ds(r, tk), :]
        v = v_ref[pl.ds(r, tk), :]
        s = _dot_nt(q, k) * scale + (cq - crow_ref[pl.ds(h, 1), pl.ds(r, tk)])
        if masked:
            rows = i * tq + lax.broadcasted_iota(jnp.int32, (tq, tk), 0)
            cols = r + lax.broadcasted_iota(jnp.int32, (tq, tk), 1)
            s = jnp.where(cols <= rows, s, NEG)
        m_prev = m_scr[...]
        m_new = jnp.maximum(m_prev, jnp.max(s, axis=-1, keepdims=True))
        alpha = jnp.exp(m_prev - m_new)
        p = jnp.exp(s - m_new)
        l_scr[...] = alpha * l_scr[...] + jnp.sum(p, axis=-1, keepdims=True)
        acc_scr[...] = alpha * acc_scr[...] + _dot(p.astype(BF16), v)
        m_scr[...] = m_new

    n_full = (i * tq) // tk

    def full_body(j, c):
        step(j, False)
        return c

    lax.fori_loop(0, n_full, full_body, 0)
    for d in range(tq // tk):
        step(n_full + d, True)
    o_ref[...] = (acc_scr[...] / l_scr[...]).astype(o_ref.dtype)


def _fox_attention(z, ccol, crow, *, batch, seq):
    m = z.shape[0]
    tq, tk = TQ, TK
    nq = seq // tq
    return pl.pallas_call(
        functools.partial(_fox_kernel, tq=tq, tk=tk, scale=HEAD_DIM ** -0.5),
        out_shape=jax.ShapeDtypeStruct((m, N_HEADS * HEAD_DIM), BF16),
        grid=(batch, N_HEADS, nq),
        in_specs=[
            pl.BlockSpec((tq, HEAD_DIM), lambda b, h, i: (b * nq + i, Z_FQ * N_HEADS + h)),
            pl.BlockSpec((seq, HEAD_DIM), lambda b, h, i: (b, Z_FK * N_HEADS + h)),
            pl.BlockSpec((seq, HEAD_DIM), lambda b, h, i: (b, Z_FV * N_HEADS + h)),
            pl.BlockSpec((tq, LANES), lambda b, h, i: (b * nq + i, 0)),
            pl.BlockSpec((8, seq), lambda b, h, i: (b * (LANES // 8), 0)),
        ],
        out_specs=pl.BlockSpec((tq, HEAD_DIM), lambda b, h, i: (b * nq + i, h)),
        scratch_shapes=[pltpu.VMEM((tq, 1), F32), pltpu.VMEM((tq, 1), F32),
                        pltpu.VMEM((tq, HEAD_DIM), F32)],
        compiler_params=_cparams("parallel", "parallel", "arbitrary"),
        name="fox_attention",
    )(z, z, z, ccol, crow)


def _causal_conv3(u, prev, w):
    rows = lax.broadcasted_iota(jnp.int32, u.shape, 0)
    np_ = prev.shape[0]
    p1 = prev[np_ - 1:np_, :]
    p2 = prev[np_ - 2:np_ - 1, :]
    s1 = jnp.where(rows == 0, p1, pltpu.roll(u, 1, 0))
    s2 = jnp.where(rows == 0, p2, jnp.where(rows == 1, p1, pltpu.roll(u, 2, 0)))
    return w[0:1, :] * s2 + w[1:2, :] * s1 + w[2:3, :] * u


def _convb_kernel(cb_ref, cc_ref, ch_ref, ccp_ref, chp_ref, w_ref, o_ref, *, seq_tiles):
    first = (pl.program_id(0) % seq_tiles) == 0
    u = cc_ref[...].astype(F32) * ch_ref[...].astype(F32)
    prev = ccp_ref[...].astype(F32) * chp_ref[...].astype(F32)
    prev = prev * jnp.where(first, 0.0, 1.0)
    y = cb_ref[...].astype(F32) * _causal_conv3(u, prev, w_ref[...])
    o_ref[...] = y.astype(o_ref.dtype)


def _conv_branch(z, conv_w, *, seq, tm):
    m = z.shape[0]
    w = N_HEADS * HEAD_DIM
    hb = tm // BF16_ROWS
    prev_map = lambda col: (lambda i: (jnp.maximum(i * hb - 1, 0), col))
    return pl.pallas_call(
        functools.partial(_convb_kernel, seq_tiles=seq // tm),
        out_shape=jax.ShapeDtypeStruct((m, w), BF16),
        grid=(m // tm,),
        in_specs=[
            pl.BlockSpec((tm, w), lambda i: (i, Z_CB)),
            pl.BlockSpec((tm, w), lambda i: (i, Z_CC)),
            pl.BlockSpec((tm, w), lambda i: (i, Z_CH)),
            pl.BlockSpec((BF16_ROWS, w), prev_map(Z_CC)),
            pl.BlockSpec((BF16_ROWS, w), prev_map(Z_CH)),
            pl.BlockSpec((3, w), lambda i: (0, 0)),
        ],
        out_specs=pl.BlockSpec((tm, w), lambda i: (i, 0)),
        compiler_params=_cparams("parallel"),
        name="conv_branch",
    )(z, z, z, z, z, conv_w)


def _retention_kernel(q_ref, k_ref, v_ref, g_ref, cos_ref, sin_ref, dm_ref, dec_ref,
                      gng_ref, gnb_ref, o_ref, s_scr, *, chunk, rows, scale):
    @pl.when(pl.program_id(2) == 0)
    def _():
        s_scr[...] = jnp.zeros(s_scr.shape, F32)

    dm = dm_ref[0]
    qd = dec_ref[0, 0:chunk, :]
    kd = dec_ref[0, chunk:2 * chunk, :]
    cd = dec_ref[0, 2 * chunk:2 * chunk + 1, :]
    half = HEAD_DIM // 2
    for n in range(rows // chunk):
        sl = pl.ds(n * chunk, chunk)
        cos = cos_ref[sl, :]
        sin = sin_ref[sl, :]
        q = q_ref[sl, :].astype(F32)
        k = k_ref[sl, :].astype(F32)
        v = v_ref[sl, :]
        qs = (q * cos + pltpu.roll(q, half, 1) * sin) * scale
        kr = k * cos + pltpu.roll(k, half, 1) * sin
        att = _dot_nt(qs.astype(BF16), kr.astype(BF16)) * dm
        intra = _dot(att.astype(BF16), v)
        state = s_scr[...]
        cross = _dot((qs * qd).astype(BF16), state.astype(BF16))
        contrib = _dot((kr * kd).T.astype(BF16), v)
        s_scr[...] = cd * state + contrib
        o = intra + cross
        mu = jnp.mean(o, axis=-1, keepdims=True)
        dlt = o - mu
        var = jnp.mean(dlt * dlt, axis=-1, keepdims=True)
        yn = dlt * lax.rsqrt(var + EPS) * gng_ref[...] + gnb_ref[...]
        gate = g_ref[sl, :].astype(F32)
        o_ref[sl, :] = (gate * _sigmoid(gate) * yn).astype(o_ref.dtype)


def _retention_tables(seq, chunk):
    half = HEAD_DIM // 2
    inv = ROPE_BASE ** (-np.arange(half, dtype=np.float64) / half)
    ang = np.arange(seq, dtype=np.float64)[:, None] * inv[None, :]
    cos2 = np.concatenate([np.cos(ang), np.cos(ang)], axis=-1)
    sin2 = np.concatenate([-np.sin(ang), np.sin(ang)], axis=-1)
    gam = 1.0 - 2.0 ** (-5.0 - np.arange(N_HEADS, dtype=np.float64))
    idx = np.arange(chunk, dtype=np.float64)
    diff = idx[:, None] - idx[None, :]
    dmask = np.where(diff >= 0, gam[:, None, None] ** np.maximum(diff, 0.0), 0.0)
    dec = np.zeros((N_HEADS, 2 * chunk + 8, LANES))
    dec[:, 0:chunk, :] = (gam[:, None] ** (idx + 1.0))[:, :, None]
    dec[:, chunk:2 * chunk, :] = (gam[:, None] ** (chunk - 1.0 - idx))[:, :, None]
    dec[:, 2 * chunk:, :] = (gam ** chunk)[:, None, None]
    f = lambda a: jnp.asarray(a.astype(np.float32))
    return f(cos2), f(sin2), f(dmask), f(dec)


def _retention(z, gn_g, gn_b, *, batch, seq):
    m = z.shape[0]
    chunk, rows = RET_CHUNK, RET_ROWS
    nr = seq // rows
    cos2, sin2, dmask, dec = _retention_tables(seq, chunk)
    w = N_HEADS * HEAD_DIM
    zspec = lambda col: pl.BlockSpec((rows, HEAD_DIM),
                                     lambda b, h, i: (b * nr + i, col * N_HEADS + h))
    return pl.pallas_call(
        functools.partial(_retention_kernel, chunk=chunk, rows=rows, scale=HEAD_DIM ** -0.5),
        out_shape=jax.ShapeDtypeStruct((m, w), BF16),
        grid=(batch, N_HEADS, nr),
        in_specs=[
            zspec(Z_RQ), zspec(Z_RK), zspec(Z_RV), zspec(Z_RG),
            pl.BlockSpec((rows, HEAD_DIM), lambda b, h, i: (i, 0)),
            pl.BlockSpec((rows, HEAD_DIM), lambda b, h, i: (i, 0)),
            pl.BlockSpec((1, chunk, chunk), lambda b, h, i: (h, 0, 0)),
            pl.BlockSpec((1, 2 * chunk + 8, LANES), lambda b, h, i: (h, 0, 0)),
            pl.BlockSpec((1, HEAD_DIM), lambda b, h, i: (0, h)),
            pl.BlockSpec((1, HEAD_DIM), lambda b, h, i: (0, h)),
        ],
        out_specs=pl.BlockSpec((rows, HEAD_DIM), lambda b, h, i: (b * nr + i, h)),
        scratch_shapes=[pltpu.VMEM((HEAD_DIM, HEAD_DIM), F32)],
        compiler_params=_cparams("parallel", "parallel", "arbitrary"),
        name="retention",
    )(z, z, z, z, cos2, sin2, dmask, dec, gn_g.reshape(1, w), gn_b.reshape(1, w))


def _merge_kernel(a_ref, b_ref, c_ref, ga_ref, gb_ref, gc_ref, bga_ref, bgb_ref, bgc_ref,
                  wa_ref, wb_ref, wc_ref, o_ref):
    def branch(x_ref, g_ref, bg_ref, w_ref):
        gate = _sigmoid(g_ref[...].astype(F32) + bg_ref[...])
        return gate * _dot(x_ref[...], w_ref[...])

    o_ref[...] = (branch(a_ref, ga_ref, bga_ref, wa_ref)
                  + branch(b_ref, gb_ref, bgb_ref, wb_ref)
                  + branch(c_ref, gc_ref, bgc_ref, wc_ref)).astype(o_ref.dtype)


def _merge(ya, yb, yc, z, b_gate, wa, wb, wc, *, tm, tn):
    m, kdim = ya.shape
    n = wa.shape[1]
    nt = n // tn
    gate_off = Z_GATES * 1024 // tn
    xspec = pl.BlockSpec((tm, kdim), lambda i, j: (i, 0))
    gspec = lambda r: pl.BlockSpec((tm, tn), lambda i, j: (i, gate_off + r * nt + j))
    bspec = lambda r: pl.BlockSpec((1, tn), lambda i, j: (0, r * nt + j))
    wspec = pl.BlockSpec((kdim, tn), lambda i, j: (0, j))
    bg = b_gate.reshape(1, 3 * n)
    return pl.pallas_call(
        _merge_kernel,
        out_shape=jax.ShapeDtypeStruct((m, n), BF16),
        grid=(m // tm, nt),
        in_specs=[xspec, xspec, xspec, gspec(0), gspec(1), gspec(2),
                  bspec(0), bspec(1), bspec(2), wspec, wspec, wspec],
        out_specs=pl.BlockSpec((tm, tn), lambda i, j: (i, j)),
        compiler_params=_cparams("parallel", "arbitrary"),
        name="gated_merge",
    )(ya, yb, yc, z, z, z, bg, bg, bg, wa, wb, wc)


def _resid_proj_kernel(x_ref, a_ref, w_ref, o_ref):
    o_ref[...] = x_ref[...] + _dot(a_ref[...], w_ref[...])


def _resid_proj(x, a, w, *, tm, tn, name):
    m, n = x.shape
    kdim = a.shape[1]
    return pl.pallas_call(
        _resid_proj_kernel,
        out_shape=jax.ShapeDtypeStruct((m, n), F32),
        grid=(m // tm, n // tn),
        in_specs=[pl.BlockSpec((tm, tn), lambda i, j: (i, j)),
                  pl.BlockSpec((tm, kdim), lambda i, j: (i, 0)),
                  pl.BlockSpec((kdim, tn), lambda i, j: (0, j))],
        out_specs=pl.BlockSpec((tm, tn), lambda i, j: (i, j)),
        compiler_params=_cparams("parallel", "arbitrary"),
        name=name,
    )(x, a, w)


def _cross_kernel(x_ref, g_ref, wq_ref, kv_ref, wo_ref, gf_ref, xo_ref, ho_ref, *, scale):
    x = x_ref[...]
    h = _rms(x, g_ref[...]).astype(BF16)
    q = _dot(h, wq_ref[...]).astype(BF16)
    cw = CROSS_HEADS * HEAD_DIM
    outs = []
    for hh in range(CROSS_HEADS):
        lo = hh * HEAD_DIM
        qh = q[:, lo:lo + HEAD_DIM]
        kh = kv_ref[:, lo:lo + HEAD_DIM]
        vh = kv_ref[:, cw + lo:cw + lo + HEAD_DIM]
        s = _dot_nt(qh, kh) * scale
        p = jnp.exp(s - jnp.max(s, axis=-1, keepdims=True))
        den = jnp.sum(p, axis=-1, keepdims=True)
        outs.append(_dot(p.astype(BF16), vh) / den)
    o = jnp.concatenate(outs, axis=-1).astype(BF16)
    x_new = x + _dot(o, wo_ref[...])
    xo_ref[...] = x_new
    ho_ref[...] = _rms(x_new, gf_ref[...]).astype(ho_ref.dtype)


def _cross_attention(x, g_cross, wq, kv, wo, g_ffn, *, seq, mem_len, tm):
    m, d = x.shape
    cw = CROSS_HEADS * HEAD_DIM
    tiles_per_seq = seq // tm
    return pl.pallas_call(
        functools.partial(_cross_kernel, scale=HEAD_DIM ** -0.5),
        out_shape=[jax.ShapeDtypeStruct((m, d), F32), jax.ShapeDtypeStruct((m, d), BF16)],
        grid=(m // tm,),
        in_specs=[
            pl.BlockSpec((tm, d), lambda i: (i, 0)),
            pl.BlockSpec((1, d), lambda i: (0, 0)),
            pl.BlockSpec((d, cw), lambda i: (0, 0)),
            pl.BlockSpec((mem_len, 2 * cw), lambda i: (i // tiles_per_seq, 0)),
            pl.BlockSpec((cw, d), lambda i: (0, 0)),
            pl.BlockSpec((1, d), lambda i: (0, 0)),
        ],
        out_specs=[pl.BlockSpec((tm, d), lambda i: (i, 0)),
                   pl.BlockSpec((tm, d), lambda i: (i, 0))],
        compiler_params=_cparams("parallel"),
        name="cross_attention",
    )(x, g_cross.reshape(1, d), wq, kv, wo, g_ffn.reshape(1, d))


def _ffn_up_kernel(h_ref, hp_ref, wa_ref, wg_ref, cwa_ref, cwg_ref, cba_ref, cbg_ref, o_ref,
                   *, seq_tiles):
    first = (pl.program_id(0) % seq_tiles) == 0
    keep = jnp.where(first, 0.0, 1.0)
    h = h_ref[...]
    hp = hp_ref[...]

    def branch(w_ref, cw_ref, cb_ref):
        w = w_ref[...]
        u = _dot(h, w)
        prev = _dot(hp, w) * keep
        return _causal_conv3(u, prev, cw_ref[...]) + cb_ref[...]

    a = branch(wa_ref, cwa_ref, cba_ref)
    g = branch(wg_ref, cwg_ref, cbg_ref)
    o_ref[...] = (g * _sigmoid(g) * a).astype(o_ref.dtype)


def _ffn_up(h, w_up, conv_w, conv_b, *, seq, tm, tn):
    m, d = h.shape
    dff = w_up.shape[1] // 2
    nt = dff // tn
    hb = tm // BF16_ROWS
    cb = conv_b.reshape(1, 2 * dff)
    return pl.pallas_call(
        functools.partial(_ffn_up_kernel, seq_tiles=seq // tm),
        out_shape=jax.ShapeDtypeStruct((m, dff), BF16),
        grid=(m // tm, nt),
        in_specs=[
            pl.BlockSpec((tm, d), lambda i, j: (i, 0)),
            pl.BlockSpec((BF16_ROWS, d), lambda i, j: (jnp.maximum(i * hb - 1, 0), 0)),
            pl.BlockSpec((d, tn), lambda i, j: (0, j)),
            pl.BlockSpec((d, tn), lambda i, j: (0, nt + j)),
            pl.BlockSpec((3, tn), lambda i, j: (0, j)),
            pl.BlockSpec((3, tn), lambda i, j: (0, nt + j)),
            pl.BlockSpec((1, tn), lambda i, j: (0, j)),
            pl.BlockSpec((1, tn), lambda i, j: (0, nt + j)),
        ],
        out_specs=pl.BlockSpec((tm, tn), lambda i, j: (i, j)),
        compiler_params=_cparams("parallel", "arbitrary"),
        name="ffn_up_conv_gate",
    )(h, h, w_up, w_up, conv_w, conv_w, cb, cb)


def _final_norm_kernel(x_ref, g_ref, o_ref):
    o_ref[...] = _rms(x_ref[...], g_ref[...])


def _final_norm(x, g, *, tm):
    m, d = x.shape
    return pl.pallas_call(
        _final_norm_kernel,
        out_shape=jax.ShapeDtypeStruct((m, d), F32),
        grid=(m // tm,),
        in_specs=[pl.BlockSpec((tm, d), lambda i: (i, 0)),
                  pl.BlockSpec((1, d), lambda i: (0, 0))],
        out_specs=pl.BlockSpec((tm, d), lambda i: (i, 0)),
        compiler_params=_cparams("parallel"),
        name="final_norm",
    )(x, g.reshape(1, d))


def _split_w_in(w_in_l):
    fw = N_HEADS * HEAD_DIM
    c0 = 3 * fw
    c1 = c0 + N_HEADS
    main = jnp.concatenate([w_in_l[:, :c0], w_in_l[:, c1:]], axis=1).astype(BF16)
    wfl = jnp.pad(w_in_l[:, c0:c1], ((0, 0), (0, LANES - N_HEADS))).astype(BF16)
    return main, wfl


def kernel(x, mem, g_mix, w_in, b_f, b_gate, conv_w, ret_gn_g, ret_gn_b, w_fox_o, w_conv_o,
           w_ret_o, w_out, g_cross, g_mem, w_cq, w_ckv, w_co, g_ffn, w_up, ffn_conv_w,
           ffn_conv_b, w_down, g_final):
    batch, seq, d = x.shape
    mem_len = mem.shape[1]
    depth = w_in.shape[0]
    m = batch * seq
    tm = min(TM, seq)
    tmc = min(TM_CROSS, seq)
    assert seq % tm == 0 and seq % TQ == 0 and seq % RET_ROWS == 0 and seq % CUM_CHUNK == 0
    assert w_in.shape[2] - N_HEADS == Z_WIDTH

    xf = x.reshape(m, d)
    memf = mem.reshape(batch * mem_len, d)
    for l in range(depth):
        w_main, w_fl = _split_w_in(w_in[l])
        z, fl = _norm_proj(xf, g_mix[l], w_main, w_fl, tm=tm, tn=TN)
        bf_pad = jnp.pad(b_f[l], (0, LANES - N_HEADS)).reshape(1, LANES)
        ccol, crow = _forget_cumsum(fl, bf_pad, batch=batch, seq=seq)
        ya = _fox_attention(z, ccol, crow, batch=batch, seq=seq)
        yb = _conv_branch(z, conv_w[l], seq=seq, tm=tm)
        yc = _retention(z, ret_gn_g[l], ret_gn_b[l], batch=batch, seq=seq)
        mg = _merge(ya, yb, yc, z, b_gate[l], w_fox_o[l].astype(BF16),
                    w_conv_o[l].astype(BF16), w_ret_o[l].astype(BF16), tm=tm, tn=TN)
        x1 = _resid_proj(xf, mg, w_out[l].astype(BF16), tm=tm, tn=TN, name="out_proj")
        kv = _norm_proj(memf, g_mem[l], w_ckv[l].astype(BF16), tm=mem_len, tn=TN)
        x2, h3 = _cross_attention(x1, g_cross[l], w_cq[l].astype(BF16), kv,
                                  w_co[l].astype(BF16), g_ffn[l], seq=seq, mem_len=mem_len,
                                  tm=tmc)
        act = _ffn_up(h3, w_up[l].astype(BF16), ffn_conv_w[l], ffn_conv_b[l],
                      seq=seq, tm=tm, tn=TN)
        xf = _resid_proj(x2, act, w_down[l].astype(BF16), tm=tm, tn=TN, name="down_proj")
    return _final_norm(xf, g_final, tm=tm).reshape(batch, seq, d)
```

```python
import functools

import numpy as np
import jax
import jax.numpy as jnp
from jax import lax
from jax.experimental import pallas as pl
from jax.experimental.pallas import tpu as pltpu

F32 = jnp.float32
BF16 = jnp.bfloat16

HEAD_DIM = 128
N_HEADS = 8
CROSS_HEADS = 4
ROPE_BASE = 10000.0
EPS = 1e-6
NEG = -1e30
LOG2E = 1.4426950408889634

LANES = 128
SUBLANES = 8
BF16_ROWS = 16
MXU_N = 256
VMEM_LIMIT = 56 * 1024 * 1024

TM = 1024
TN = 512
TN_IN = 1024
TM_CROSS = 512
TQ = 1024
TK = 1024
CUM_CHUNK = 256
RET_CHUNK = 256
RET_ROWS = 1024

Z_FQ, Z_FK, Z_FV, Z_CB, Z_CC, Z_CH, Z_RQ, Z_RK, Z_RV, Z_RG = range(10)
Z_GATES = 10
Z_WIDTH = 16 * 1024


def _cparams(*sem):
    return pltpu.CompilerParams(dimension_semantics=sem, vmem_limit_bytes=VMEM_LIMIT)


def _rms(x, g):
    ms = jnp.mean(x * x, axis=-1, keepdims=True)
    return x * lax.rsqrt(ms + EPS) * g


def _sigmoid(x):
    return 1.0 / (1.0 + jnp.exp(-x))


def _dot(a, b):
    return jnp.dot(a, b, preferred_element_type=F32)


def _dot_nt(a, b):
    return lax.dot_general(a, b, (((1,), (1,)), ((), ())), preferred_element_type=F32)


def _norm_proj_kernel(x_ref, g_ref, w_ref, *rest, with_fl):
    if with_fl:
        wfl_ref, z_ref, fl_ref, h_scr = rest
    else:
        z_ref, h_scr = rest

    @pl.when(pl.program_id(1) == 0)
    def _():
        h = _rms(x_ref[...], g_ref[...]).astype(BF16)
        h_scr[...] = h
        if with_fl:
            fl_ref[...] = _dot(h, wfl_ref[...])

    z_ref[...] = _dot(h_scr[...], w_ref[...]).astype(z_ref.dtype)


def _norm_proj(x, g, w, wfl=None, *, tm, tn):
    m, d = x.shape
    n = w.shape[1]
    with_fl = wfl is not None
    in_specs = [
        pl.BlockSpec((tm, d), lambda i, j: (i, 0)),
        pl.BlockSpec((1, d), lambda i, j: (0, 0)),
        pl.BlockSpec((d, tn), lambda i, j: (0, j)),
    ]
    out_shape = [jax.ShapeDtypeStruct((m, n), BF16)]
    out_specs = [pl.BlockSpec((tm, tn), lambda i, j: (i, j))]
    args = [x, g.reshape(1, d), w]
    if with_fl:
        in_specs.append(pl.BlockSpec((d, LANES), lambda i, j: (0, 0)))
        out_shape.append(jax.ShapeDtypeStruct((m, LANES), F32))
        out_specs.append(pl.BlockSpec((tm, LANES), lambda i, j: (i, 0)))
        args.append(wfl)
    res = pl.pallas_call(
        functools.partial(_norm_proj_kernel, with_fl=with_fl),
        out_shape=out_shape,
        grid=(m // tm, n // tn),
        in_specs=in_specs,
        out_specs=out_specs,
        scratch_shapes=[pltpu.VMEM((tm, d), BF16)],
        compiler_params=_cparams("parallel", "arbitrary"),
        name="norm_proj_fl" if with_fl else "norm_proj",
    )(*args)
    return res if with_fl else res[0]


def _forget_cumsum_kernel(fl_ref, bf_ref, ccol_ref, crow_ref, *, chunk):
    n = fl_ref.shape[0] // chunk
    r_i = lax.broadcasted_iota(jnp.int32, (chunk, chunk), 0)
    c_i = lax.broadcasted_iota(jnp.int32, (chunk, chunk), 1)
    tri = (r_i >= c_i).astype(BF16)

    def body(c, carry):
        r = pl.multiple_of(c * chunk, chunk)
        t = fl_ref[pl.ds(r, chunk), :] + bf_ref[...]
        ls = jnp.minimum(t, 0.0) - jnp.log1p(jnp.exp(-jnp.abs(t)))
        hi = ls.astype(BF16)
        r1 = ls - hi.astype(F32)
        mid = r1.astype(BF16)
        lo = (r1 - mid.astype(F32)).astype(BF16)
        cs = _dot(tri, hi) + _dot(tri, mid) + _dot(tri, lo) + carry
        ccol_ref[pl.ds(r, chunk), :] = cs
        crow_ref[:, pl.ds(r, chunk)] = cs.T
        return cs[chunk - 1:chunk, :]

    lax.fori_loop(0, n, body, jnp.zeros((1, LANES), F32))


def _forget_cumsum(fl, bf_pad, *, batch, seq):
    m = fl.shape[0]
    return pl.pallas_call(
        functools.partial(_forget_cumsum_kernel, chunk=CUM_CHUNK),
        out_shape=[jax.ShapeDtypeStruct((m, LANES), F32),
                   jax.ShapeDtypeStruct((batch * LANES, seq), F32)],
        grid=(batch,),
        in_specs=[pl.BlockSpec((seq, LANES), lambda b: (b, 0)),
                  pl.BlockSpec((1, LANES), lambda b: (0, 0))],
        out_specs=[pl.BlockSpec((seq, LANES), lambda b: (b, 0)),
                   pl.BlockSpec((LANES, seq), lambda b: (b, 0))],
        compiler_params=_cparams("parallel"),
        name="forget_cumsum",
    )(fl, bf_pad)


def _fox_kernel(q_ref, k_ref, v_ref, ccol_ref, crow_ref, o_ref, m_scr, l_scr, acc_scr,
                *, tq, tk, scale):
    h = pl.program_id(1)
    i = pl.program_id(2)
    nb = tk // LANES
    q2 = (q_ref[...].astype(F32) * (scale * LOG2E)).astype(BF16)
    lane = lax.broadcasted_iota(jnp.int32, (tq, LANES), 1)
    cq = jnp.sum(jnp.where(lane == h, ccol_ref[...], 0.0), axis=-1, keepdims=True) * LOG2E
    cq_b = jnp.broadcast_to(cq, (tq, LANES))

    m_scr[...] = jnp.full(m_scr.shape, NEG, F32)
    l_scr[...] = jnp.zeros(l_scr.shape, F32)
    acc_scr[...] = jnp.zeros(acc_scr.shape, F32)

    def step(j, masked):
        r = pl.multiple_of(j * tk, tk)
        k = k_ref[pl.ds(r, tk), :]
        v = v_ref[pl.ds(r, tk), :]
        ck = crow_ref[pl.ds(h, 1), pl.ds(r, tk)] * LOG2E
        u = _dot_nt(q2, k) - ck
        if masked:
            rows = i * tq + lax.broadcasted_iota(jnp.int32, (tq, tk), 0)
            cols = r + lax.broadcasted_iota(jnp.int32, (tq, tk), 1)
            u = jnp.where(cols <= rows, u, NEG)
        ub = [u[:, c * LANES:(c + 1) * LANES] for c in range(nb)]
        umax = functools.reduce(jnp.maximum, ub)
        m_prev = m_scr[...]
        m_new = jnp.maximum(m_prev, jnp.max(umax, axis=-1, keepdims=True) + cq_b)
        alpha = jnp.exp2(m_prev - m_new)
        shift = m_new - cq_b
        pb = [jnp.exp2(b - shift) for b in ub]
        l_scr[...] = alpha * l_scr[...] + functools.reduce(jnp.add, pb)
        p = jnp.concatenate([b.astype(BF16) for b in pb], axis=-1)
        acc_scr[...] = alpha * acc_scr[...] + _dot(p, v)
        m_scr[...] = m_new

    n_full = (i * tq) // tk

    def pair_body(j, c):
        step(2 * j, False)
        step(2 * j + 1, False)
        return c

    lax.fori_loop(0, n_full // 2, pair_body, 0)

    @pl.when(n_full % 2 == 1)
    def _():
        step(n_full - 1, False)

    for d in range(tq // tk):
        step(n_full + d, True)
    den = jnp.sum(l_scr[...], axis=-1, keepdims=True)
    o_ref[...] = (acc_scr[...] * (1.0 / den)).astype(o_ref.dtype)


def _fox_attention(z, ccol, crow, *, batch, seq):
    m = z.shape[0]
    tq, tk = TQ, TK
    nq = seq // tq
    return pl.pallas_call(
        functools.partial(_fox_kernel, tq=tq, tk=tk, scale=HEAD_DIM ** -0.5),
        out_shape=jax.ShapeDtypeStruct((m, N_HEADS * HEAD_DIM), BF16),
        grid=(batch, N_HEADS, nq),
        in_specs=[
            pl.BlockSpec((tq, HEAD_DIM), lambda b, h, i: (b * nq + i, Z_FQ * N_HEADS + h)),
            pl.BlockSpec((seq, HEAD_DIM), lambda b, h, i: (b, Z_FK * N_HEADS + h)),
            pl.BlockSpec((seq, HEAD_DIM), lambda b, h, i: (b, Z_FV * N_HEADS + h)),
            pl.BlockSpec((tq, LANES), lambda b, h, i: (b * nq + i, 0)),
            pl.BlockSpec((8, seq), lambda b, h, i: (b * (LANES // 8), 0)),
        ],
        out_specs=pl.BlockSpec((tq, HEAD_DIM), lambda b, h, i: (b * nq + i, h)),
        scratch_shapes=[pltpu.VMEM((tq, LANES), F32), pltpu.VMEM((tq, LANES), F32),
                        pltpu.VMEM((tq, HEAD_DIM), F32)],
        compiler_params=_cparams("parallel", "parallel", "arbitrary"),
        name="fox_attention",
    )(z, z, z, ccol, crow)


def _causal_conv3(u, prev, w):
    tm, tn = u.shape
    u3 = u.reshape(tm // SUBLANES, SUBLANES, tn)
    p3 = prev[prev.shape[0] - SUBLANES:, :].reshape(1, SUBLANES, tn)
    ext = jnp.concatenate([p3, u3], axis=0)
    sub = lax.broadcasted_iota(jnp.int32, u3.shape, 1)
    r1 = pltpu.roll(ext, 1, 1)
    r2 = pltpu.roll(ext, 2, 1)
    s1 = jnp.where(sub < 1, r1[:-1], r1[1:])
    s2 = jnp.where(sub < 2, r2[:-1], r2[1:])
    y = w[0:1, :] * s2 + w[1:2, :] * s1 + w[2:3, :] * u3
    return y.reshape(tm, tn)


def _convb_kernel(cb_ref, cc_ref, ch_ref, ccp_ref, chp_ref, w_ref, o_ref, *, seq_tiles):
    first = (pl.program_id(0) % seq_tiles) == 0
    u = cc_ref[...].astype(F32) * ch_ref[...].astype(F32)
    prev = ccp_ref[...].astype(F32) * chp_ref[...].astype(F32)
    prev = prev * jnp.where(first, 0.0, 1.0)
    y = cb_ref[...].astype(F32) * _causal_conv3(u, prev, w_ref[...])
    o_ref[...] = y.astype(o_ref.dtype)


def _conv_branch(z, conv_w, *, seq, tm):
    m = z.shape[0]
    w = N_HEADS * HEAD_DIM
    hb = tm // BF16_ROWS
    prev_map = lambda col: (lambda i: (jnp.maximum(i * hb - 1, 0), col))
    return pl.pallas_call(
        functools.partial(_convb_kernel, seq_tiles=seq // tm),
        out_shape=jax.ShapeDtypeStruct((m, w), BF16),
        grid=(m // tm,),
        in_specs=[
            pl.BlockSpec((tm, w), lambda i: (i, Z_CB)),
            pl.BlockSpec((tm, w), lambda i: (i, Z_CC)),
            pl.BlockSpec((tm, w), lambda i: (i, Z_CH)),
            pl.BlockSpec((BF16_ROWS, w), prev_map(Z_CC)),
            pl.BlockSpec((BF16_ROWS, w), prev_map(Z_CH)),
            pl.BlockSpec((3, w), lambda i: (0, 0)),
        ],
        out_specs=pl.BlockSpec((tm, w), lambda i: (i, 0)),
        compiler_params=_cparams("parallel"),
        name="conv_branch",
    )(z, z, z, z, z, conv_w)


def _retention_kernel(q_ref, k_ref, v_ref, g_ref, cos_ref, sin_ref, dm_ref, dec_ref,
                      gng_ref, gnb_ref, o_ref, s_scr, *, chunk, rows, scale):
    @pl.when(pl.program_id(2) == 0)
    def _():
        s_scr[...] = jnp.zeros(s_scr.shape, F32)

    dm = dm_ref[0]
    qd = dec_ref[0, 0:chunk, :]
    kd = dec_ref[0, chunk:2 * chunk, :]
    cd = dec_ref[0, 2 * chunk:2 * chunk + 1, :]
    half = HEAD_DIM // 2
    for n in range(rows // chunk):
        sl = pl.ds(n * chunk, chunk)
        cos = cos_ref[sl, :]
        sin = sin_ref[sl, :]
        q = q_ref[sl, :].astype(F32)
        k = k_ref[sl, :].astype(F32)
        v = v_ref[sl, :]
        qs = (q * cos + pltpu.roll(q, half, 1) * sin) * scale
        kr = k * cos + pltpu.roll(k, half, 1) * sin
        att = _dot_nt(qs.astype(BF16), kr.astype(BF16)) * dm
        intra = _dot(att.astype(BF16), v)
        state = s_scr[...]
        cross = _dot((qs * qd).astype(BF16), state.astype(BF16))
        contrib = _dot((kr * kd).T.astype(BF16), v)
        s_scr[...] = cd * state + contrib
        o = intra + cross
        mu = jnp.mean(o, axis=-1, keepdims=True)
        dlt = o - mu
        var = jnp.mean(dlt * dlt, axis=-1, keepdims=True)
        yn = dlt * lax.rsqrt(var + EPS) * gng_ref[...] + gnb_ref[...]
        gate = g_ref[sl, :].astype(F32)
        o_ref[sl, :] = (gate * _sigmoid(gate) * yn).astype(o_ref.dtype)


def _retention_tables(seq, chunk):
    half = HEAD_DIM // 2
    inv = ROPE_BASE ** (-np.arange(half, dtype=np.float64) / half)
    ang = np.arange(seq, dtype=np.float64)[:, None] * inv[None, :]
    cos2 = np.concatenate([np.cos(ang), np.cos(ang)], axis=-1)
    sin2 = np.concatenate([-np.sin(ang), np.sin(ang)], axis=-1)
    gam = 1.0 - 2.0 ** (-5.0 - np.arange(N_HEADS, dtype=np.float64))
    idx = np.arange(chunk, dtype=np.float64)
    diff = idx[:, None] - idx[None, :]
    dmask = np.where(diff >= 0, gam[:, None, None] ** np.maximum(diff, 0.0), 0.0)
    dec = np.zeros((N_HEADS, 2 * chunk + 8, LANES))
    dec[:, 0:chunk, :] = (gam[:, None] ** (idx + 1.0))[:, :, None]
    dec[:, chunk:2 * chunk, :] = (gam[:, None] ** (chunk - 1.0 - idx))[:, :, None]
    dec[:, 2 * chunk:, :] = (gam ** chunk)[:, None, None]
    f = lambda a: jnp.asarray(a.astype(np.float32))
    return f(cos2), f(sin2), f(dmask), f(dec)


def _retention(z, gn_g, gn_b, *, batch, seq):
    m = z.shape[0]
    chunk, rows = RET_CHUNK, RET_ROWS
    nr = seq // rows
    cos2, sin2, dmask, dec = _retention_tables(seq, chunk)
    w = N_HEADS * HEAD_DIM
    zspec = lambda col: pl.BlockSpec((rows, HEAD_DIM),
                                     lambda b, h, i: (b * nr + i, col * N_HEADS + h))
    return pl.pallas_call(
        functools.partial(_retention_kernel, chunk=chunk, rows=rows, scale=HEAD_DIM ** -0.5),
        out_shape=jax.ShapeDtypeStruct((m, w), BF16),
        grid=(batch, N_HEADS, nr),
        in_specs=[
            zspec(Z_RQ), zspec(Z_RK), zspec(Z_RV), zspec(Z_RG),
            pl.BlockSpec((rows, HEAD_DIM), lambda b, h, i: (i, 0)),
            pl.BlockSpec((rows, HEAD_DIM), lambda b, h, i: (i, 0)),
            pl.BlockSpec((1, chunk, chunk), lambda b, h, i: (h, 0, 0)),
            pl.BlockSpec((1, 2 * chunk + 8, LANES), lambda b, h, i: (h, 0, 0)),
            pl.BlockSpec((1, HEAD_DIM), lambda b, h, i: (0, h)),
            pl.BlockSpec((1, HEAD_DIM), lambda b, h, i: (0, h)),
        ],
        out_specs=pl.BlockSpec((rows, HEAD_DIM), lambda b, h, i: (b * nr + i, h)),
        scratch_shapes=[pltpu.VMEM((HEAD_DIM, HEAD_DIM), F32)],
        compiler_params=_cparams("parallel", "parallel", "arbitrary"),
        name="retention",
    )(z, z, z, z, cos2, sin2, dmask, dec, gn_g.reshape(1, w), gn_b.reshape(1, w))


def _merge_kernel(a_ref, b_ref, c_ref, ga_ref, gb_ref, gc_ref, bga_ref, bgb_ref, bgc_ref,
                  wa_ref, wb_ref, wc_ref, o_ref):
    def branch(x_ref, g_ref, bg_ref, w_ref):
        gate = _sigmoid(g_ref[...].astype(F32) + bg_ref[...])
        return gate * _dot(x_ref[...], w_ref[...])

    o_ref[...] = (branch(a_ref, ga_ref, bga_ref, wa_ref)
                  + branch(b_ref, gb_ref, bgb_ref, wb_ref)
                  + branch(c_ref, gc_ref, bgc_ref, wc_ref)).astype(o_ref.dtype)


def _merge(ya, yb, yc, z, b_gate, wa, wb, wc, *, tm, tn):
    m, kdim = ya.shape
    n = wa.shape[1]
    nt = n // tn
    gate_off = Z_GATES * 1024 // tn
    xspec = pl.BlockSpec((tm, kdim), lambda i, j: (i, 0))
    gspec = lambda r: pl.BlockSpec((tm, tn), lambda i, j: (i, gate_off + r * nt + j))
    bspec = lambda r: pl.BlockSpec((1, tn), lambda i, j: (0, r * nt + j))
    wspec = pl.BlockSpec((kdim, tn), lambda i, j: (0, j))
    bg = b_gate.reshape(1, 3 * n)
    return pl.pallas_call(
        _merge_kernel,
        out_shape=jax.ShapeDtypeStruct((m, n), BF16),
        grid=(m // tm, nt),
        in_specs=[xspec, xspec, xspec, gspec(0), gspec(1), gspec(2),
                  bspec(0), bspec(1), bspec(2), wspec, wspec, wspec],
        out_specs=pl.BlockSpec((tm, tn), lambda i, j: (i, j)),
        compiler_params=_cparams("parallel", "arbitrary"),
        name="gated_merge",
    )(ya, yb, yc, z, z, z, bg, bg, bg, wa, wb, wc)


def _resid_proj_kernel(x_ref, a_ref, w_ref, o_ref):
    o_ref[...] = x_ref[...] + _dot(a_ref[...], w_ref[...])


def _resid_proj(x, a, w, *, tm, tn, name):
    m, n = x.shape
    kdim = a.shape[1]
    return pl.pallas_call(
        _resid_proj_kernel,
        out_shape=jax.ShapeDtypeStruct((m, n), F32),
        grid=(m // tm, n // tn),
        in_specs=[pl.BlockSpec((tm, tn), lambda i, j: (i, j)),
                  pl.BlockSpec((tm, kdim), lambda i, j: (i, 0)),
                  pl.BlockSpec((kdim, tn), lambda i, j: (0, j))],
        out_specs=pl.BlockSpec((tm, tn), lambda i, j: (i, j)),
        compiler_params=_cparams("parallel", "arbitrary"),
        name=name,
    )(x, a, w)


def _cross_kernel(x_ref, g_ref, wq_ref, kv_ref, wo_ref, gf_ref, xo_ref, ho_ref, *, scale):
    x = x_ref[...]
    h = _rms(x, g_ref[...]).astype(BF16)
    q = _dot(h, wq_ref[...]).astype(BF16)
    cw = CROSS_HEADS * HEAD_DIM
    outs = []
    for hh in range(CROSS_HEADS):
        lo = hh * HEAD_DIM
        qh = q[:, lo:lo + HEAD_DIM]
        kh = kv_ref[:, lo:lo + HEAD_DIM]
        vh = kv_ref[:, cw + lo:cw + lo + HEAD_DIM]
        s = _dot_nt(qh, kh) * scale
        p = jnp.exp(s - jnp.max(s, axis=-1, keepdims=True))
        den = jnp.sum(p, axis=-1, keepdims=True)
        outs.append(_dot(p.astype(BF16), vh) / den)
    o = jnp.concatenate(outs, axis=-1).astype(BF16)
    x_new = x + _dot(o, wo_ref[...])
    xo_ref[...] = x_new
    ho_ref[...] = _rms(x_new, gf_ref[...]).astype(ho_ref.dtype)


def _cross_attention(x, g_cross, wq, kv, wo, g_ffn, *, seq, mem_len, tm):
    m, d = x.shape
    cw = CROSS_HEADS * HEAD_DIM
    tiles_per_seq = seq // tm
    return pl.pallas_call(
        functools.partial(_cross_kernel, scale=HEAD_DIM ** -0.5),
        out_shape=[jax.ShapeDtypeStruct((m, d), F32), jax.ShapeDtypeStruct((m, d), BF16)],
        grid=(m // tm,),
        in_specs=[
            pl.BlockSpec((tm, d), lambda i: (i, 0)),
            pl.BlockSpec((1, d), lambda i: (0, 0)),
            pl.BlockSpec((d, cw), lambda i: (0, 0)),
            pl.BlockSpec((mem_len, 2 * cw), lambda i: (i // tiles_per_seq, 0)),
            pl.BlockSpec((cw, d), lambda i: (0, 0)),
            pl.BlockSpec((1, d), lambda i: (0, 0)),
        ],
        out_specs=[pl.BlockSpec((tm, d), lambda i: (i, 0)),
                   pl.BlockSpec((tm, d), lambda i: (i, 0))],
        compiler_params=_cparams("parallel"),
        name="cross_attention",
    )(x, g_cross.reshape(1, d), wq, kv, wo, g_ffn.reshape(1, d))


def _ffn_up_kernel(h_ref, hp_ref, wa_ref, wg_ref, cwa_ref, cwg_ref, cba_ref, cbg_ref, o_ref,
                   *, seq_tiles):
    first = (pl.program_id(0) % seq_tiles) == 0
    keep = jnp.where(first, 0.0, 1.0)
    h = h_ref[...]
    hp = hp_ref[...]

    def branch(w_ref, cw_ref, cb_ref, cols):
        w = w_ref[:, cols]
        u = _dot(h, w)
        prev = _dot(hp, w) * keep
        return _causal_conv3(u, prev, cw_ref[:, cols]) + cb_ref[:, cols]

    pieces = [slice(c, c + MXU_N) for c in range(0, o_ref.shape[1], MXU_N)]
    gates = []
    for cols in pieces:
        g = branch(wg_ref, cwg_ref, cbg_ref, cols)
        gates.append(g * _sigmoid(g))
    for cols, gate in zip(pieces, gates):
        a = branch(wa_ref, cwa_ref, cba_ref, cols)
        o_ref[:, cols] = (gate * a).astype(o_ref.dtype)


def _ffn_up(h, w_up, conv_w, conv_b, *, seq, tm, tn):
    m, d = h.shape
    dff = w_up.shape[1] // 2
    nt = dff // tn
    hb = tm // BF16_ROWS
    cb = conv_b.reshape(1, 2 * dff)
    return pl.pallas_call(
        functools.partial(_ffn_up_kernel, seq_tiles=seq // tm),
        out_shape=jax.ShapeDtypeStruct((m, dff), BF16),
        grid=(m // tm, nt),
        in_specs=[
            pl.BlockSpec((tm, d), lambda i, j: (i, 0)),
            pl.BlockSpec((BF16_ROWS, d), lambda i, j: (jnp.maximum(i * hb - 1, 0), 0)),
            pl.BlockSpec((d, tn), lambda i, j: (0, j)),
            pl.BlockSpec((d, tn), lambda i, j: (0, nt + j)),
            pl.BlockSpec((3, tn), lambda i, j: (0, j)),
            pl.BlockSpec((3, tn), lambda i, j: (0, nt + j)),
            pl.BlockSpec((1, tn), lambda i, j: (0, j)),
            pl.BlockSpec((1, tn), lambda i, j: (0, nt + j)),
        ],
        out_specs=pl.BlockSpec((tm, tn), lambda i, j: (i, j)),
        compiler_params=_cparams("parallel", "arbitrary"),
        name="ffn_up_conv_gate",
    )(h, h, w_up, w_up, conv_w, conv_w, cb, cb)


def _final_norm_kernel(x_ref, g_ref, o_ref):
    o_ref[...] = _rms(x_ref[...], g_ref[...])


def _final_norm(x, g, *, tm):
    m, d = x.shape
    return pl.pallas_call(
        _final_norm_kernel,
        out_shape=jax.ShapeDtypeStruct((m, d), F32),
        grid=(m // tm,),
        in_specs=[pl.BlockSpec((tm, d), lambda i: (i, 0)),
                  pl.BlockSpec((1, d), lambda i: (0, 0))],
        out_specs=pl.BlockSpec((tm, d), lambda i: (i, 0)),
        compiler_params=_cparams("parallel"),
        name="final_norm",
    )(x, g.reshape(1, d))


def _split_w_in(w_in_l):
    fw = N_HEADS * HEAD_DIM
    c0 = 3 * fw
    c1 = c0 + N_HEADS
    main = jnp.concatenate([w_in_l[:, :c0], w_in_l[:, c1:]], axis=1).astype(BF16)
    wfl = jnp.pad(w_in_l[:, c0:c1], ((0, 0), (0, LANES - N_HEADS))).astype(BF16)
    return main, wfl


def kernel(x, mem, g_mix, w_in, b_f, b_gate, conv_w, ret_gn_g, ret_gn_b, w_fox_o, w_conv_o,
           w_ret_o, w_out, g_cross, g_mem, w_cq, w_ckv, w_co, g_ffn, w_up, ffn_conv_w,
           ffn_conv_b, w_down, g_final):
    batch, seq, d = x.shape
    mem_len = mem.shape[1]
    depth = w_in.shape[0]
    m = batch * seq
    tm = min(TM, seq)
    tmc = min(TM_CROSS, seq)
    assert seq % tm == 0 and seq % TQ == 0 and seq % RET_ROWS == 0 and seq % CUM_CHUNK == 0
    assert w_in.shape[2] - N_HEADS == Z_WIDTH

    xf = x.reshape(m, d)
    memf = mem.reshape(batch * mem_len, d)
    for l in range(depth):
        w_main, w_fl = _split_w_in(w_in[l])
        z, fl = _norm_proj(xf, g_mix[l], w_main, w_fl, tm=tm, tn=TN_IN)
        bf_pad = jnp.pad(b_f[l], (0, LANES - N_HEADS)).reshape(1, LANES)
        ccol, crow = _forget_cumsum(fl, bf_pad, batch=batch, seq=seq)
        ya = _fox_attention(z, ccol, crow, batch=batch, seq=seq)
        yb = _conv_branch(z, conv_w[l], seq=seq, tm=tm)
        yc = _retention(z, ret_gn_g[l], ret_gn_b[l], batch=batch, seq=seq)
        mg = _merge(ya, yb, yc, z, b_gate[l], w_fox_o[l].astype(BF16),
                    w_conv_o[l].astype(BF16), w_ret_o[l].astype(BF16), tm=tm, tn=TN)
        x1 = _resid_proj(xf, mg, w_out[l].astype(BF16), tm=tm, tn=TN, name="out_proj")
        kv = _norm_proj(memf, g_mem[l], w_ckv[l].astype(BF16), tm=mem_len, tn=TN)
        x2, h3 = _cross_attention(x1, g_cross[l], w_cq[l].astype(BF16), kv,
                                  w_co[l].astype(BF16), g_ffn[l], seq=seq, mem_len=mem_len,
                                  tm=tmc)
        act = _ffn_up(h3, w_up[l].astype(BF16), ffn_conv_w[l], ffn_conv_b[l],
                      seq=seq, tm=tm, tn=TN)
        xf = _resid_proj(x2, act, w_down[l].astype(BF16), tm=tm, tn=TN, name="down_proj")
    return _final_norm(xf, g_final, tm=tm).reshape(batch, seq, d)
```

```python
import functools

import numpy as np
import jax
import jax.numpy as jnp
from jax import lax
from jax.experimental import pallas as pl
from jax.experimental.pallas import tpu as pltpu

F32 = jnp.float32
BF16 = jnp.bfloat16

HEAD_DIM = 128
N_HEADS = 8
CROSS_HEADS = 4
ROPE_BASE = 10000.0
EPS = 1e-6
NEG = -1e30
LOG2E = 1.4426950408889634

LANES = 128
SUBLANES = 8
BF16_ROWS = 16
MXU_N = 256
VMEM_LIMIT = 56 * 1024 * 1024

TM = 1024
TN = 512
TN_IN = 1024
TM_CROSS = 512
TQ = 1024
TK = 1024
CUM_CHUNK = 256
RET_CHUNK = 256
RET_ROWS = 2048

Z_FQ, Z_FK, Z_FV, Z_CB, Z_CC, Z_CH, Z_RQ, Z_RK, Z_RV, Z_RG = range(10)
Z_GATES = 10
Z_WIDTH = 16 * 1024


def _cparams(*sem):
    return pltpu.CompilerParams(dimension_semantics=sem, vmem_limit_bytes=VMEM_LIMIT)


def _rms(x, g):
    ms = jnp.mean(x * x, axis=-1, keepdims=True)
    return x * lax.rsqrt(ms + EPS) * g


def _sigmoid(x):
    return 1.0 / (1.0 + jnp.exp(-x))


def _dot(a, b):
    return jnp.dot(a, b, preferred_element_type=F32)


def _dot_nt(a, b):
    return lax.dot_general(a, b, (((1,), (1,)), ((), ())), preferred_element_type=F32)


def _norm_proj_kernel(x_ref, g_ref, w_ref, *rest, with_fl):
    if with_fl:
        wfl_ref, z_ref, fl_ref, h_scr = rest
    else:
        z_ref, h_scr = rest

    @pl.when(pl.program_id(1) == 0)
    def _():
        h = _rms(x_ref[...], g_ref[...]).astype(BF16)
        h_scr[...] = h
        if with_fl:
            fl_ref[...] = _dot(h, wfl_ref[...])

    z_ref[...] = _dot(h_scr[...], w_ref[...]).astype(z_ref.dtype)


def _norm_proj(x, g, w, wfl=None, *, tm, tn):
    m, d = x.shape
    n = w.shape[1]
    with_fl = wfl is not None
    in_specs = [
        pl.BlockSpec((tm, d), lambda i, j: (i, 0)),
        pl.BlockSpec((1, d), lambda i, j: (0, 0)),
        pl.BlockSpec((d, tn), lambda i, j: (0, j)),
    ]
    out_shape = [jax.ShapeDtypeStruct((m, n), BF16)]
    out_specs = [pl.BlockSpec((tm, tn), lambda i, j: (i, j))]
    args = [x, g.reshape(1, d), w]
    if with_fl:
        in_specs.append(pl.BlockSpec((d, LANES), lambda i, j: (0, 0)))
        out_shape.append(jax.ShapeDtypeStruct((m, LANES), F32))
        out_specs.append(pl.BlockSpec((tm, LANES), lambda i, j: (i, 0)))
        args.append(wfl)
    res = pl.pallas_call(
        functools.partial(_norm_proj_kernel, with_fl=with_fl),
        out_shape=out_shape,
        grid=(m // tm, n // tn),
        in_specs=in_specs,
        out_specs=out_specs,
        scratch_shapes=[pltpu.VMEM((tm, d), BF16)],
        compiler_params=_cparams("parallel", "arbitrary"),
        name="norm_proj_fl" if with_fl else "norm_proj",
    )(*args)
    return res if with_fl else res[0]


def _forget_cumsum_kernel(fl_ref, bf_ref, ccol_ref, crow_ref, *, chunk):
    n = fl_ref.shape[0] // chunk
    r_i = lax.broadcasted_iota(jnp.int32, (chunk, chunk), 0)
    c_i = lax.broadcasted_iota(jnp.int32, (chunk, chunk), 1)
    tri = (r_i >= c_i).astype(BF16)

    def body(c, carry):
        r = pl.multiple_of(c * chunk, chunk)
        t = fl_ref[pl.ds(r, chunk), :] + bf_ref[...]
        ls = jnp.minimum(t, 0.0) - jnp.log1p(jnp.exp(-jnp.abs(t)))
        hi = ls.astype(BF16)
        r1 = ls - hi.astype(F32)
        mid = r1.astype(BF16)
        lo = (r1 - mid.astype(F32)).astype(BF16)
        cs = _dot(tri, hi) + _dot(tri, mid) + _dot(tri, lo) + carry
        ccol_ref[pl.ds(r, chunk), :] = cs
        crow_ref[:, pl.ds(r, chunk)] = cs.T
        return cs[chunk - 1:chunk, :]

    lax.fori_loop(0, n, body, jnp.zeros((1, LANES), F32))


def _forget_cumsum(fl, bf_pad, *, batch, seq):
    m = fl.shape[0]
    return pl.pallas_call(
        functools.partial(_forget_cumsum_kernel, chunk=CUM_CHUNK),
        out_shape=[jax.ShapeDtypeStruct((m, LANES), F32),
                   jax.ShapeDtypeStruct((batch * LANES, seq), F32)],
        grid=(batch,),
        in_specs=[pl.BlockSpec((seq, LANES), lambda b: (b, 0)),
                  pl.BlockSpec((1, LANES), lambda b: (0, 0))],
        out_specs=[pl.BlockSpec((seq, LANES), lambda b: (b, 0)),
                   pl.BlockSpec((LANES, seq), lambda b: (b, 0))],
        compiler_params=_cparams("parallel"),
        name="forget_cumsum",
    )(fl, bf_pad)


def _fox_kernel(q_ref, k_ref, v_ref, ccol_ref, crow_ref, o_ref, m_scr, l_scr, acc_scr,
                *, tq, tk, scale):
    h = pl.program_id(1)
    i = pl.program_id(2)
    q2 = (q_ref[...].astype(F32) * (scale * LOG2E)).astype(BF16)
    lane = lax.broadcasted_iota(jnp.int32, (tq, LANES), 1)
    cq = jnp.sum(jnp.where(lane == h, ccol_ref[...], 0.0), axis=-1, keepdims=True) * LOG2E
    cq_b = jnp.broadcast_to(cq, (tq, LANES))

    m_scr[...] = jnp.full(m_scr.shape, NEG, F32)
    l_scr[...] = jnp.zeros(l_scr.shape, F32)
    acc_scr[...] = jnp.zeros(acc_scr.shape, F32)

    def block(r0, nr, k0, nk, masked):
        rs = pl.ds(r0, nr)
        k = k_ref[pl.ds(k0, nk), :]
        v = v_ref[pl.ds(k0, nk), :]
        ck = crow_ref[pl.ds(h, 1), pl.ds(k0, nk)] * LOG2E
        cqs = cq_b[r0:r0 + nr]
        u = _dot_nt(q2[r0:r0 + nr], k) - ck
        if masked:
            rows = (i * tq + r0) + lax.broadcasted_iota(jnp.int32, (nr, nk), 0)
            cols = k0 + lax.broadcasted_iota(jnp.int32, (nr, nk), 1)
            u = jnp.where(cols <= rows, u, NEG)
        ub = [u[:, c * LANES:(c + 1) * LANES] for c in range(nk // LANES)]
        umax = functools.reduce(jnp.maximum, ub)
        m_prev = m_scr[rs, :]
        m_new = jnp.maximum(m_prev, jnp.max(umax, axis=-1, keepdims=True) + cqs)
        alpha = jnp.exp2(m_prev - m_new)
        shift = m_new - cqs
        pb = [jnp.exp2(b - shift) for b in ub]
        l_scr[rs, :] = alpha * l_scr[rs, :] + functools.reduce(jnp.add, pb)
        p = jnp.concatenate([b.astype(BF16) for b in pb], axis=-1)
        acc_scr[rs, :] = alpha * acc_scr[rs, :] + _dot(p, v)
        m_scr[rs, :] = m_new

    def step(j):
        block(0, tq, pl.multiple_of(j * tk, tk), tk, False)

    n_full = (i * tq) // tk

    def pair_body(j, c):
        step(2 * j)
        step(2 * j + 1)
        return c

    lax.fori_loop(0, n_full // 2, pair_body, 0)

    @pl.when(n_full % 2 == 1)
    def _():
        step(n_full - 1)

    kd = pl.multiple_of(i * tq, tq)
    half = tq // 2
    block(0, half, kd, half, True)
    block(half, half, kd, tq, True)
    den = jnp.sum(l_scr[...], axis=-1, keepdims=True)
    o_ref[...] = (acc_scr[...] * (1.0 / den)).astype(o_ref.dtype)


def _fox_attention(z, ccol, crow, *, batch, seq):
    m = z.shape[0]
    tq, tk = TQ, TK
    assert tq % tk == 0 and tq % (2 * LANES) == 0
    nq = seq // tq
    return pl.pallas_call(
        functools.partial(_fox_kernel, tq=tq, tk=tk, scale=HEAD_DIM ** -0.5),
        out_shape=jax.ShapeDtypeStruct((m, N_HEADS * HEAD_DIM), BF16),
        grid=(batch, N_HEADS, nq),
        in_specs=[
            pl.BlockSpec((tq, HEAD_DIM), lambda b, h, i: (b * nq + i, Z_FQ * N_HEADS + h)),
            pl.BlockSpec((seq, HEAD_DIM), lambda b, h, i: (b, Z_FK * N_HEADS + h)),
            pl.BlockSpec((seq, HEAD_DIM), lambda b, h, i: (b, Z_FV * N_HEADS + h)),
            pl.BlockSpec((tq, LANES), lambda b, h, i: (b * nq + i, 0)),
            pl.BlockSpec((8, seq), lambda b, h, i: (b * (LANES // 8), 0)),
        ],
        out_specs=pl.BlockSpec((tq, HEAD_DIM), lambda b, h, i: (b * nq + i, h)),
        scratch_shapes=[pltpu.VMEM((tq, LANES), F32), pltpu.VMEM((tq, LANES), F32),
                        pltpu.VMEM((tq, HEAD_DIM), F32)],
        compiler_params=_cparams("parallel", "parallel", "arbitrary"),
        name="fox_attention",
    )(z, z, z, ccol, crow)


def _causal_conv3(u, prev, w):
    tm, tn = u.shape
    u3 = u.reshape(tm // SUBLANES, SUBLANES, tn)
    p3 = prev[prev.shape[0] - SUBLANES:, :].reshape(1, SUBLANES, tn)
    ext = jnp.concatenate([p3, u3], axis=0)
    sub = lax.broadcasted_iota(jnp.int32, u3.shape, 1)
    r1 = pltpu.roll(ext, 1, 1)
    r2 = pltpu.roll(ext, 2, 1)
    s1 = jnp.where(sub < 1, r1[:-1], r1[1:])
    s2 = jnp.where(sub < 2, r2[:-1], r2[1:])
    y = w[0:1, :] * s2 + w[1:2, :] * s1 + w[2:3, :] * u3
    return y.reshape(tm, tn)


def _convb_kernel(cb_ref, cc_ref, ch_ref, ccp_ref, chp_ref, w_ref, o_ref, *, seq_tiles):
    first = (pl.program_id(0) % seq_tiles) == 0
    u = cc_ref[...].astype(F32) * ch_ref[...].astype(F32)
    prev = ccp_ref[...].astype(F32) * chp_ref[...].astype(F32)
    prev = prev * jnp.where(first, 0.0, 1.0)
    y = cb_ref[...].astype(F32) * _causal_conv3(u, prev, w_ref[...])
    o_ref[...] = y.astype(o_ref.dtype)


def _conv_branch(z, conv_w, *, seq, tm):
    m = z.shape[0]
    w = N_HEADS * HEAD_DIM
    hb = tm // BF16_ROWS
    prev_map = lambda col: (lambda i: (jnp.maximum(i * hb - 1, 0), col))
    return pl.pallas_call(
        functools.partial(_convb_kernel, seq_tiles=seq // tm),
        out_shape=jax.ShapeDtypeStruct((m, w), BF16),
        grid=(m // tm,),
        in_specs=[
            pl.BlockSpec((tm, w), lambda i: (i, Z_CB)),
            pl.BlockSpec((tm, w), lambda i: (i, Z_CC)),
            pl.BlockSpec((tm, w), lambda i: (i, Z_CH)),
            pl.BlockSpec((BF16_ROWS, w), prev_map(Z_CC)),
            pl.BlockSpec((BF16_ROWS, w), prev_map(Z_CH)),
            pl.BlockSpec((3, w), lambda i: (0, 0)),
        ],
        out_specs=pl.BlockSpec((tm, w), lambda i: (i, 0)),
        compiler_params=_cparams("parallel"),
        name="conv_branch",
    )(z, z, z, z, z, conv_w)


def _retention_kernel(q_ref, k_ref, v_ref, g_ref, cos_ref, sin_ref, dm_ref, dec_ref,
                      gng_ref, gnb_ref, o_ref, s_scr, *, chunk, rows, scale):
    @pl.when(pl.program_id(2) == 0)
    def _():
        s_scr[...] = jnp.zeros(s_scr.shape, F32)

    dm = dm_ref[0]
    qd = dec_ref[0, 0:chunk, :]
    kd = dec_ref[0, chunk:2 * chunk, :]
    cd = dec_ref[0, 2 * chunk:2 * chunk + 1, :]
    half = HEAD_DIM // 2
    for n in range(rows // chunk):
        sl = pl.ds(n * chunk, chunk)
        cos = cos_ref[sl, :]
        sin = sin_ref[sl, :]
        q = q_ref[sl, :].astype(F32)
        k = k_ref[sl, :].astype(F32)
        v = v_ref[sl, :]
        qs = (q * cos + pltpu.roll(q, half, 1) * sin) * scale
        kr = k * cos + pltpu.roll(k, half, 1) * sin
        att = _dot_nt(qs.astype(BF16), kr.astype(BF16)) * dm
        intra = _dot(att.astype(BF16), v)
        state = s_scr[...]
        cross = _dot((qs * qd).astype(BF16), state.astype(BF16))
        contrib = _dot((kr * kd).T.astype(BF16), v)
        s_scr[...] = cd * state + contrib
        o = intra + cross
        mu = jnp.mean(o, axis=-1, keepdims=True)
        dlt = o - mu
        var = jnp.mean(dlt * dlt, axis=-1, keepdims=True)
        yn = dlt * lax.rsqrt(var + EPS) * gng_ref[...] + gnb_ref[...]
        gate = g_ref[sl, :].astype(F32)
        o_ref[sl, :] = (gate * _sigmoid(gate) * yn).astype(o_ref.dtype)


def _retention_tables(seq, chunk):
    half = HEAD_DIM // 2
    inv = ROPE_BASE ** (-np.arange(half, dtype=np.float64) / half)
    ang = np.arange(seq, dtype=np.float64)[:, None] * inv[None, :]
    cos2 = np.concatenate([np.cos(ang), np.cos(ang)], axis=-1)
    sin2 = np.concatenate([-np.sin(ang), np.sin(ang)], axis=-1)
    gam = 1.0 - 2.0 ** (-5.0 - np.arange(N_HEADS, dtype=np.float64))
    idx = np.arange(chunk, dtype=np.float64)
    diff = idx[:, None] - idx[None, :]
    dmask = np.where(diff >= 0, gam[:, None, None] ** np.maximum(diff, 0.0), 0.0)
    dec = np.zeros((N_HEADS, 2 * chunk + 8, LANES))
    dec[:, 0:chunk, :] = (gam[:, None] ** (idx + 1.0))[:, :, None]
    dec[:, chunk:2 * chunk, :] = (gam[:, None] ** (chunk - 1.0 - idx))[:, :, None]
    dec[:, 2 * chunk:, :] = (gam ** chunk)[:, None, None]
    f = lambda a: jnp.asarray(a.astype(np.float32))
    return f(cos2), f(sin2), f(dmask), f(dec)


def _retention(z, gn_g, gn_b, *, batch, seq):
    m = z.shape[0]
    chunk, rows = RET_CHUNK, RET_ROWS
    nr = seq // rows
    cos2, sin2, dmask, dec = _retention_tables(seq, chunk)
    w = N_HEADS * HEAD_DIM
    zspec = lambda col: pl.BlockSpec((rows, HEAD_DIM),
                                     lambda b, h, i: (b * nr + i, col * N_HEADS + h))
    return pl.pallas_call(
        functools.partial(_retention_kernel, chunk=chunk, rows=rows, scale=HEAD_DIM ** -0.5),
        out_shape=jax.ShapeDtypeStruct((m, w), BF16),
        grid=(batch, N_HEADS, nr),
        in_specs=[
            zspec(Z_RQ), zspec(Z_RK), zspec(Z_RV), zspec(Z_RG),
            pl.BlockSpec((rows, HEAD_DIM), lambda b, h, i: (i, 0)),
            pl.BlockSpec((rows, HEAD_DIM), lambda b, h, i: (i, 0)),
            pl.BlockSpec((1, chunk, chunk), lambda b, h, i: (h, 0, 0)),
            pl.BlockSpec((1, 2 * chunk + 8, LANES), lambda b, h, i: (h, 0, 0)),
            pl.BlockSpec((1, HEAD_DIM), lambda b, h, i: (0, h)),
            pl.BlockSpec((1, HEAD_DIM), lambda b, h, i: (0, h)),
        ],
        out_specs=pl.BlockSpec((rows, HEAD_DIM), lambda b, h, i: (b * nr + i, h)),
        scratch_shapes=[pltpu.VMEM((HEAD_DIM, HEAD_DIM), F32)],
        compiler_params=_cparams("parallel", "parallel", "arbitrary"),
        name="retention",
    )(z, z, z, z, cos2, sin2, dmask, dec, gn_g.reshape(1, w), gn_b.reshape(1, w))


def _merge_kernel(a_ref, b_ref, c_ref, ga_ref, gb_ref, gc_ref, bga_ref, bgb_ref, bgc_ref,
                  wa_ref, wb_ref, wc_ref, o_ref):
    def branch(x_ref, g_ref, bg_ref, w_ref):
        gate = _sigmoid(g_ref[...].astype(F32) + bg_ref[...])
        return gate * _dot(x_ref[...], w_ref[...])

    o_ref[...] = (branch(a_ref, ga_ref, bga_ref, wa_ref)
                  + branch(b_ref, gb_ref, bgb_ref, wb_ref)
                  + branch(c_ref, gc_ref, bgc_ref, wc_ref)).astype(o_ref.dtype)


def _merge(ya, yb, yc, z, b_gate, wa, wb, wc, *, tm, tn):
    m, kdim = ya.shape
    n = wa.shape[1]
    nt = n // tn
    gate_off = Z_GATES * 1024 // tn
    xspec = pl.BlockSpec((tm, kdim), lambda i, j: (i, 0))
    gspec = lambda r: pl.BlockSpec((tm, tn), lambda i, j: (i, gate_off + r * nt + j))
    bspec = lambda r: pl.BlockSpec((1, tn), lambda i, j: (0, r * nt + j))
    wspec = pl.BlockSpec((kdim, tn), lambda i, j: (0, j))
    bg = b_gate.reshape(1, 3 * n)
    return pl.pallas_call(
        _merge_kernel,
        out_shape=jax.ShapeDtypeStruct((m, n), BF16),
        grid=(m // tm, nt),
        in_specs=[xspec, xspec, xspec, gspec(0), gspec(1), gspec(2),
                  bspec(0), bspec(1), bspec(2), wspec, wspec, wspec],
        out_specs=pl.BlockSpec((tm, tn), lambda i, j: (i, j)),
        compiler_params=_cparams("parallel", "arbitrary"),
        name="gated_merge",
    )(ya, yb, yc, z, z, z, bg, bg, bg, wa, wb, wc)


def _resid_proj_kernel(x_ref, a_ref, w_ref, o_ref):
    o_ref[...] = x_ref[...] + _dot(a_ref[...], w_ref[...])


def _resid_proj(x, a, w, *, tm, tn, name):
    m, n = x.shape
    kdim = a.shape[1]
    return pl.pallas_call(
        _resid_proj_kernel,
        out_shape=jax.ShapeDtypeStruct((m, n), F32),
        grid=(m // tm, n // tn),
        in_specs=[pl.BlockSpec((tm, tn), lambda i, j: (i, j)),
                  pl.BlockSpec((tm, kdim), lambda i, j: (i, 0)),
                  pl.BlockSpec((kdim, tn), lambda i, j: (0, j))],
        out_specs=pl.BlockSpec((tm, tn), lambda i, j: (i, j)),
        compiler_params=_cparams("parallel", "arbitrary"),
        name=name,
    )(x, a, w)


def _cross_kernel(x_ref, g_ref, wq_ref, kv_ref, wo_ref, gf_ref, xo_ref, ho_ref, *, scale):
    x = x_ref[...]
    h = _rms(x, g_ref[...]).astype(BF16)
    q = _dot(h, wq_ref[...]).astype(BF16)
    cw = CROSS_HEADS * HEAD_DIM
    outs = []
    for hh in range(CROSS_HEADS):
        lo = hh * HEAD_DIM
        qh = q[:, lo:lo + HEAD_DIM]
        kh = kv_ref[:, lo:lo + HEAD_DIM]
        vh = kv_ref[:, cw + lo:cw + lo + HEAD_DIM]
        s = _dot_nt(qh, kh) * scale
        p = jnp.exp(s - jnp.max(s, axis=-1, keepdims=True))
        den = jnp.sum(p, axis=-1, keepdims=True)
        outs.append(_dot(p.astype(BF16), vh) / den)
    o = jnp.concatenate(outs, axis=-1).astype(BF16)
    x_new = x + _dot(o, wo_ref[...])
    xo_ref[...] = x_new
    ho_ref[...] = _rms(x_new, gf_ref[...]).astype(ho_ref.dtype)


def _cross_attention(x, g_cross, wq, kv, wo, g_ffn, *, seq, mem_len, tm):
    m, d = x.shape
    cw = CROSS_HEADS * HEAD_DIM
    tiles_per_seq = seq // tm
    return pl.pallas_call(
        functools.partial(_cross_kernel, scale=HEAD_DIM ** -0.5),
        out_shape=[jax.ShapeDtypeStruct((m, d), F32), jax.ShapeDtypeStruct((m, d), BF16)],
        grid=(m // tm,),
        in_specs=[
            pl.BlockSpec((tm, d), lambda i: (i, 0)),
            pl.BlockSpec((1, d), lambda i: (0, 0)),
            pl.BlockSpec((d, cw), lambda i: (0, 0)),
            pl.BlockSpec((mem_len, 2 * cw), lambda i: (i // tiles_per_seq, 0)),
            pl.BlockSpec((cw, d), lambda i: (0, 0)),
            pl.BlockSpec((1, d), lambda i: (0, 0)),
        ],
        out_specs=[pl.BlockSpec((tm, d), lambda i: (i, 0)),
                   pl.BlockSpec((tm, d), lambda i: (i, 0))],
        compiler_params=_cparams("parallel"),
        name="cross_attention",
    )(x, g_cross.reshape(1, d), wq, kv, wo, g_ffn.reshape(1, d))


def _ffn_up_kernel(h_ref, hp_ref, wa_ref, wg_ref, cwa_ref, cwg_ref, cba_ref, cbg_ref, o_ref,
                   *, seq_tiles):
    first = (pl.program_id(0) % seq_tiles) == 0
    keep = jnp.where(first, 0.0, 1.0)
    h = h_ref[...]
    hp = hp_ref[...]

    def branch(w_ref, cw_ref, cb_ref, cols):
        w = w_ref[:, cols]
        u = _dot(h, w)
        prev = _dot(hp, w) * keep
        return _causal_conv3(u, prev, cw_ref[:, cols]) + cb_ref[:, cols]

    pieces = [slice(c, c + MXU_N) for c in range(0, o_ref.shape[1], MXU_N)]
    gates = []
    for cols in pieces:
        g = branch(wg_ref, cwg_ref, cbg_ref, cols)
        gates.append(g * _sigmoid(g))
    for cols, gate in zip(pieces, gates):
        a = branch(wa_ref, cwa_ref, cba_ref, cols)
        o_ref[:, cols] = (gate * a).astype(o_ref.dtype)


def _ffn_up(h, w_up, conv_w, conv_b, *, seq, tm, tn):
    m, d = h.shape
    dff = w_up.shape[1] // 2
    nt = dff // tn
    hb = tm // BF16_ROWS
    cb = conv_b.reshape(1, 2 * dff)
    return pl.pallas_call(
        functools.partial(_ffn_up_kernel, seq_tiles=seq // tm),
        out_shape=jax.ShapeDtypeStruct((m, dff), BF16),
        grid=(m // tm, nt),
        in_specs=[
            pl.BlockSpec((tm, d), lambda i, j: (i, 0)),
            pl.BlockSpec((BF16_ROWS, d), lambda i, j: (jnp.maximum(i * hb - 1, 0), 0)),
            pl.BlockSpec((d, tn), lambda i, j: (0, j)),
            pl.BlockSpec((d, tn), lambda i, j: (0, nt + j)),
            pl.BlockSpec((3, tn), lambda i, j: (0, j)),
            pl.BlockSpec((3, tn), lambda i, j: (0, nt + j)),
            pl.BlockSpec((1, tn), lambda i, j: (0, j)),
            pl.BlockSpec((1, tn), lambda i, j: (0, nt + j)),
        ],
        out_specs=pl.BlockSpec((tm, tn), lambda i, j: (i, j)),
        compiler_params=_cparams("parallel", "arbitrary"),
        name="ffn_up_conv_gate",
    )(h, h, w_up, w_up, conv_w, conv_w, cb, cb)


def _final_norm_kernel(x_ref, g_ref, o_ref):
    o_ref[...] = _rms(x_ref[...], g_ref[...])


def _final_norm(x, g, *, tm):
    m, d = x.shape
    return pl.pallas_call(
        _final_norm_kernel,
        out_shape=jax.ShapeDtypeStruct((m, d), F32),
        grid=(m // tm,),
        in_specs=[pl.BlockSpec((tm, d), lambda i: (i, 0)),
                  pl.BlockSpec((1, d), lambda i: (0, 0))],
        out_specs=pl.BlockSpec((tm, d), lambda i: (i, 0)),
        compiler_params=_cparams("parallel"),
        name="final_norm",
    )(x, g.reshape(1, d))


def _split_w_in(w_in_l):
    fw = N_HEADS * HEAD_DIM
    c0 = 3 * fw
    c1 = c0 + N_HEADS
    main = jnp.concatenate([w_in_l[:, :c0], w_in_l[:, c1:]], axis=1).astype(BF16)
    wfl = jnp.pad(w_in_l[:, c0:c1], ((0, 0), (0, LANES - N_HEADS))).astype(BF16)
    return main, wfl


def kernel(x, mem, g_mix, w_in, b_f, b_gate, conv_w, ret_gn_g, ret_gn_b, w_fox_o, w_conv_o,
           w_ret_o, w_out, g_cross, g_mem, w_cq, w_ckv, w_co, g_ffn, w_up, ffn_conv_w,
           ffn_conv_b, w_down, g_final):
    batch, seq, d = x.shape
    mem_len = mem.shape[1]
    depth = w_in.shape[0]
    m = batch * seq
    tm = min(TM, seq)
    tmc = min(TM_CROSS, seq)
    assert seq % tm == 0 and seq % TQ == 0 and seq % RET_ROWS == 0 and seq % CUM_CHUNK == 0
    assert w_in.shape[2] - N_HEADS == Z_WIDTH

    xf = x.reshape(m, d)
    memf = mem.reshape(batch * mem_len, d)
    for l in range(depth):
        w_main, w_fl = _split_w_in(w_in[l])
        z, fl = _norm_proj(xf, g_mix[l], w_main, w_fl, tm=tm, tn=TN_IN)
        bf_pad = jnp.pad(b_f[l], (0, LANES - N_HEADS)).reshape(1, LANES)
        ccol, crow = _forget_cumsum(fl, bf_pad, batch=batch, seq=seq)
        ya = _fox_attention(z, ccol, crow, batch=batch, seq=seq)
        yb = _conv_branch(z, conv_w[l], seq=seq, tm=tm)
        yc = _retention(z, ret_gn_g[l], ret_gn_b[l], batch=batch, seq=seq)
        mg = _merge(ya, yb, yc, z, b_gate[l], w_fox_o[l].astype(BF16),
                    w_conv_o[l].astype(BF16), w_ret_o[l].astype(BF16), tm=tm, tn=TN)
        x1 = _resid_proj(xf, mg, w_out[l].astype(BF16), tm=tm, tn=TN, name="out_proj")
        kv = _norm_proj(memf, g_mem[l], w_ckv[l].astype(BF16), tm=mem_len, tn=TN)
        x2, h3 = _cross_attention(x1, g_cross[l], w_cq[l].astype(BF16), kv,
                                  w_co[l].astype(BF16), g_ffn[l], seq=seq, mem_len=mem_len,
                                  tm=tmc)
        act = _ffn_up(h3, w_up[l].astype(BF16), ffn_conv_w[l], ffn_conv_b[l],
                      seq=seq, tm=tm, tn=TN)
        xf = _resid_proj(x2, act, w_down[l].astype(BF16), tm=tm, tn=TN, name="down_proj")
    return _final_norm(xf, g_final, tm=tm).reshape(batch, seq, d)
```

```python
import functools

import numpy as np
import jax
import jax.numpy as jnp
from jax import lax
from jax.experimental import pallas as pl
from jax.experimental.pallas import tpu as pltpu

F32 = jnp.float32
BF16 = jnp.bfloat16

HEAD_DIM = 128
N_HEADS = 8
CROSS_HEADS = 4
ROPE_BASE = 10000.0
EPS = 1e-6
NEG = -1e30
LOG2E = 1.4426950408889634

LANES = 128
SUBLANES = 8
BF16_ROWS = 16
MXU_N = 256
VMEM_LIMIT = 56 * 1024 * 1024

TM = 1024
TN = 512
TN_IN = 2048
TM_CROSS = 512
TQ = 1024
TK = 1024
CUM_CHUNK = 256
RET_CHUNK = 256
RET_ROWS = 2048

Z_FQ, Z_FK, Z_FV, Z_CB, Z_CC, Z_CH, Z_RQ, Z_RK, Z_RV, Z_RG = range(10)
Z_GATES = 10
Z_WIDTH = 16 * 1024


def _cparams(*sem):
    return pltpu.CompilerParams(dimension_semantics=sem, vmem_limit_bytes=VMEM_LIMIT)


def _rms(x, g):
    ms = jnp.mean(x * x, axis=-1, keepdims=True)
    return x * lax.rsqrt(ms + EPS) * g


def _sigmoid(x):
    return 1.0 / (1.0 + jnp.exp(-x))


def _dot(a, b):
    return jnp.dot(a, b, preferred_element_type=F32)


def _dot_nt(a, b):
    return lax.dot_general(a, b, (((1,), (1,)), ((), ())), preferred_element_type=F32)


def _norm_proj_kernel(x_ref, g_ref, w_ref, *rest, with_fl):
    if with_fl:
        wfl_ref, z_ref, fl_ref, h_scr = rest
    else:
        z_ref, h_scr = rest

    @pl.when(pl.program_id(1) == 0)
    def _():
        h = _rms(x_ref[...], g_ref[...]).astype(BF16)
        h_scr[...] = h
        if with_fl:
            fl_ref[...] = _dot(h, wfl_ref[...])

    z_ref[...] = _dot(h_scr[...], w_ref[...]).astype(z_ref.dtype)


def _norm_proj(x, g, w, wfl=None, *, tm, tn):
    m, d = x.shape
    n = w.shape[1]
    with_fl = wfl is not None
    in_specs = [
        pl.BlockSpec((tm, d), lambda i, j: (i, 0)),
        pl.BlockSpec((1, d), lambda i, j: (0, 0)),
        pl.BlockSpec((d, tn), lambda i, j: (0, j)),
    ]
    out_shape = [jax.ShapeDtypeStruct((m, n), BF16)]
    out_specs = [pl.BlockSpec((tm, tn), lambda i, j: (i, j))]
    args = [x, g.reshape(1, d), w]
    if with_fl:
        in_specs.append(pl.BlockSpec((d, LANES), lambda i, j: (0, 0)))
        out_shape.append(jax.ShapeDtypeStruct((m, LANES), F32))
        out_specs.append(pl.BlockSpec((tm, LANES), lambda i, j: (i, 0)))
        args.append(wfl)
    res = pl.pallas_call(
        functools.partial(_norm_proj_kernel, with_fl=with_fl),
        out_shape=out_shape,
        grid=(m // tm, n // tn),
        in_specs=in_specs,
        out_specs=out_specs,
        scratch_shapes=[pltpu.VMEM((tm, d), BF16)],
        compiler_params=_cparams("parallel", "arbitrary"),
        name="norm_proj_fl" if with_fl else "norm_proj",
    )(*args)
    return res if with_fl else res[0]


def _forget_cumsum_kernel(fl_ref, bf_ref, ccol_ref, crow_ref, *, chunk):
    n = fl_ref.shape[0] // chunk
    r_i = lax.broadcasted_iota(jnp.int32, (chunk, chunk), 0)
    c_i = lax.broadcasted_iota(jnp.int32, (chunk, chunk), 1)
    tri = (r_i >= c_i).astype(BF16)

    def body(c, carry):
        r = pl.multiple_of(c * chunk, chunk)
        t = fl_ref[pl.ds(r, chunk), :] + bf_ref[...]
        ls = jnp.minimum(t, 0.0) - jnp.log1p(jnp.exp(-jnp.abs(t)))
        hi = ls.astype(BF16)
        r1 = ls - hi.astype(F32)
        mid = r1.astype(BF16)
        lo = (r1 - mid.astype(F32)).astype(BF16)
        cs = _dot(tri, hi) + _dot(tri, mid) + _dot(tri, lo) + carry
        ccol_ref[pl.ds(r, chunk), :] = cs
        crow_ref[:, pl.ds(r, chunk)] = cs.T
        return cs[chunk - 1:chunk, :]

    lax.fori_loop(0, n, body, jnp.zeros((1, LANES), F32))


def _forget_cumsum(fl, bf_pad, *, batch, seq):
    m = fl.shape[0]
    return pl.pallas_call(
        functools.partial(_forget_cumsum_kernel, chunk=CUM_CHUNK),
        out_shape=[jax.ShapeDtypeStruct((m, LANES), F32),
                   jax.ShapeDtypeStruct((batch * LANES, seq), F32)],
        grid=(batch,),
        in_specs=[pl.BlockSpec((seq, LANES), lambda b: (b, 0)),
                  pl.BlockSpec((1, LANES), lambda b: (0, 0))],
        out_specs=[pl.BlockSpec((seq, LANES), lambda b: (b, 0)),
                   pl.BlockSpec((LANES, seq), lambda b: (b, 0))],
        compiler_params=_cparams("parallel"),
        name="forget_cumsum",
    )(fl, bf_pad)


def _fox_kernel(q_ref, k_ref, v_ref, ccol_ref, crow_ref, o_ref, m_scr, l_scr, acc_scr,
                *, tq, tk, scale):
    h = pl.program_id(1)
    i = pl.program_id(2)
    q2 = (q_ref[...].astype(F32) * (scale * LOG2E)).astype(BF16)
    lane = lax.broadcasted_iota(jnp.int32, (tq, LANES), 1)
    cq = jnp.sum(jnp.where(lane == h, ccol_ref[...], 0.0), axis=-1, keepdims=True) * LOG2E
    cq_b = jnp.broadcast_to(cq, (tq, LANES))

    m_scr[...] = jnp.full(m_scr.shape, NEG, F32)
    l_scr[...] = jnp.zeros(l_scr.shape, F32)
    acc_scr[...] = jnp.zeros(acc_scr.shape, F32)

    def block(r0, nr, k0, nk, masked):
        rs = pl.ds(r0, nr)
        k = k_ref[pl.ds(k0, nk), :]
        v = v_ref[pl.ds(k0, nk), :]
        ck = crow_ref[pl.ds(h, 1), pl.ds(k0, nk)] * LOG2E
        cqs = cq_b[r0:r0 + nr]
        u = _dot_nt(q2[r0:r0 + nr], k) - ck
        if masked:
            rows = (i * tq + r0) + lax.broadcasted_iota(jnp.int32, (nr, nk), 0)
            cols = k0 + lax.broadcasted_iota(jnp.int32, (nr, nk), 1)
            u = jnp.where(cols <= rows, u, NEG)
        ub = [u[:, c * LANES:(c + 1) * LANES] for c in range(nk // LANES)]
        umax = functools.reduce(jnp.maximum, ub)
        m_prev = m_scr[rs, :]
        m_new = jnp.maximum(m_prev, jnp.max(umax, axis=-1, keepdims=True) + cqs)
        alpha = jnp.exp2(m_prev - m_new)
        shift = m_new - cqs
        pb = [jnp.exp2(b - shift) for b in ub]
        l_scr[rs, :] = alpha * l_scr[rs, :] + functools.reduce(jnp.add, pb)
        p = jnp.concatenate([b.astype(BF16) for b in pb], axis=-1)
        acc_scr[rs, :] = alpha * acc_scr[rs, :] + _dot(p, v)
        m_scr[rs, :] = m_new

    def step(j):
        block(0, tq, pl.multiple_of(j * tk, tk), tk, False)

    n_full = (i * tq) // tk

    def pair_body(j, c):
        step(2 * j)
        step(2 * j + 1)
        return c

    lax.fori_loop(0, n_full // 2, pair_body, 0)

    @pl.when(n_full % 2 == 1)
    def _():
        step(n_full - 1)

    kd = pl.multiple_of(i * tq, tq)
    half = tq // 2
    block(0, half, kd, half, True)
    block(half, half, kd, tq, True)
    den = jnp.sum(l_scr[...], axis=-1, keepdims=True)
    o_ref[...] = (acc_scr[...] * (1.0 / den)).astype(o_ref.dtype)


def _fox_attention(z, ccol, crow, *, batch, seq):
    m = z.shape[0]
    tq, tk = TQ, TK
    assert tq % tk == 0 and tq % (2 * LANES) == 0
    nq = seq // tq
    return pl.pallas_call(
        functools.partial(_fox_kernel, tq=tq, tk=tk, scale=HEAD_DIM ** -0.5),
        out_shape=jax.ShapeDtypeStruct((m, N_HEADS * HEAD_DIM), BF16),
        grid=(batch, N_HEADS, nq),
        in_specs=[
            pl.BlockSpec((tq, HEAD_DIM), lambda b, h, i: (b * nq + i, Z_FQ * N_HEADS + h)),
            pl.BlockSpec((seq, HEAD_DIM), lambda b, h, i: (b, Z_FK * N_HEADS + h)),
            pl.BlockSpec((seq, HEAD_DIM), lambda b, h, i: (b, Z_FV * N_HEADS + h)),
            pl.BlockSpec((tq, LANES), lambda b, h, i: (b * nq + i, 0)),
            pl.BlockSpec((8, seq), lambda b, h, i: (b * (LANES // 8), 0)),
        ],
        out_specs=pl.BlockSpec((tq, HEAD_DIM), lambda b, h, i: (b * nq + i, h)),
        scratch_shapes=[pltpu.VMEM((tq, LANES), F32), pltpu.VMEM((tq, LANES), F32),
                        pltpu.VMEM((tq, HEAD_DIM), F32)],
        compiler_params=_cparams("parallel", "parallel", "arbitrary"),
        name="fox_attention",
    )(z, z, z, ccol, crow)


def _causal_conv3(u, prev, w):
    tm, tn = u.shape
    u3 = u.reshape(tm // SUBLANES, SUBLANES, tn)
    p3 = prev[prev.shape[0] - SUBLANES:, :].reshape(1, SUBLANES, tn)
    ext = jnp.concatenate([p3, u3], axis=0)
    sub = lax.broadcasted_iota(jnp.int32, u3.shape, 1)
    r1 = pltpu.roll(ext, 1, 1)
    r2 = pltpu.roll(ext, 2, 1)
    s1 = jnp.where(sub < 1, r1[:-1], r1[1:])
    s2 = jnp.where(sub < 2, r2[:-1], r2[1:])
    y = w[0:1, :] * s2 + w[1:2, :] * s1 + w[2:3, :] * u3
    return y.reshape(tm, tn)


def _convb_kernel(cb_ref, cc_ref, ch_ref, ccp_ref, chp_ref, w_ref, o_ref, *, seq_tiles):
    first = (pl.program_id(0) % seq_tiles) == 0
    u = cc_ref[...].astype(F32) * ch_ref[...].astype(F32)
    prev = ccp_ref[...].astype(F32) * chp_ref[...].astype(F32)
    prev = prev * jnp.where(first, 0.0, 1.0)
    y = cb_ref[...].astype(F32) * _causal_conv3(u, prev, w_ref[...])
    o_ref[...] = y.astype(o_ref.dtype)


def _conv_branch(z, conv_w, *, seq, tm):
    m = z.shape[0]
    w = N_HEADS * HEAD_DIM
    hb = tm // BF16_ROWS
    prev_map = lambda col: (lambda i: (jnp.maximum(i * hb - 1, 0), col))
    return pl.pallas_call(
        functools.partial(_convb_kernel, seq_tiles=seq // tm),
        out_shape=jax.ShapeDtypeStruct((m, w), BF16),
        grid=(m // tm,),
        in_specs=[
            pl.BlockSpec((tm, w), lambda i: (i, Z_CB)),
            pl.BlockSpec((tm, w), lambda i: (i, Z_CC)),
            pl.BlockSpec((tm, w), lambda i: (i, Z_CH)),
            pl.BlockSpec((BF16_ROWS, w), prev_map(Z_CC)),
            pl.BlockSpec((BF16_ROWS, w), prev_map(Z_CH)),
            pl.BlockSpec((3, w), lambda i: (0, 0)),
        ],
        out_specs=pl.BlockSpec((tm, w), lambda i: (i, 0)),
        compiler_params=_cparams("parallel"),
        name="conv_branch",
    )(z, z, z, z, z, conv_w)


def _retention_kernel(q_ref, k_ref, v_ref, g_ref, cos_ref, sin_ref, dm_ref, dec_ref,
                      gng_ref, gnb_ref, o_ref, s_scr, *, chunk, rows, scale):
    @pl.when(pl.program_id(2) == 0)
    def _():
        s_scr[...] = jnp.zeros(s_scr.shape, F32)

    dm = dm_ref[0]
    qd = dec_ref[0, 0:chunk, :]
    kd = dec_ref[0, chunk:2 * chunk, :]
    cd = dec_ref[0, 2 * chunk:2 * chunk + 1, :]
    half = HEAD_DIM // 2
    for n in range(rows // chunk):
        sl = pl.ds(n * chunk, chunk)
        cos = cos_ref[sl, :]
        sin = sin_ref[sl, :]
        q = q_ref[sl, :].astype(F32)
        k = k_ref[sl, :].astype(F32)
        v = v_ref[sl, :]
        qs = (q * cos + pltpu.roll(q, half, 1) * sin) * scale
        kr = k * cos + pltpu.roll(k, half, 1) * sin
        att = _dot_nt(qs.astype(BF16), kr.astype(BF16)) * dm
        intra = _dot(att.astype(BF16), v)
        state = s_scr[...]
        cross = _dot((qs * qd).astype(BF16), state.astype(BF16))
        contrib = _dot((kr * kd).T.astype(BF16), v)
        s_scr[...] = cd * state + contrib
        o = intra + cross
        mu = jnp.mean(o, axis=-1, keepdims=True)
        dlt = o - mu
        var = jnp.mean(dlt * dlt, axis=-1, keepdims=True)
        yn = dlt * lax.rsqrt(var + EPS) * gng_ref[...] + gnb_ref[...]
        gate = g_ref[sl, :].astype(F32)
        o_ref[sl, :] = (gate * _sigmoid(gate) * yn).astype(o_ref.dtype)


def _retention_tables(seq, chunk):
    half = HEAD_DIM // 2
    inv = ROPE_BASE ** (-np.arange(half, dtype=np.float64) / half)
    ang = np.arange(seq, dtype=np.float64)[:, None] * inv[None, :]
    cos2 = np.concatenate([np.cos(ang), np.cos(ang)], axis=-1)
    sin2 = np.concatenate([-np.sin(ang), np.sin(ang)], axis=-1)
    gam = 1.0 - 2.0 ** (-5.0 - np.arange(N_HEADS, dtype=np.float64))
    idx = np.arange(chunk, dtype=np.float64)
    diff = idx[:, None] - idx[None, :]
    dmask = np.where(diff >= 0, gam[:, None, None] ** np.maximum(diff, 0.0), 0.0)
    dec = np.zeros((N_HEADS, 2 * chunk + 8, LANES))
    dec[:, 0:chunk, :] = (gam[:, None] ** (idx + 1.0))[:, :, None]
    dec[:, chunk:2 * chunk, :] = (gam[:, None] ** (chunk - 1.0 - idx))[:, :, None]
    dec[:, 2 * chunk:, :] = (gam ** chunk)[:, None, None]
    f = lambda a: jnp.asarray(a.astype(np.float32))
    return f(cos2), f(sin2), f(dmask), f(dec)


def _retention(z, gn_g, gn_b, *, batch, seq):
    m = z.shape[0]
    chunk, rows = RET_CHUNK, RET_ROWS
    nr = seq // rows
    cos2, sin2, dmask, dec = _retention_tables(seq, chunk)
    w = N_HEADS * HEAD_DIM
    zspec = lambda col: pl.BlockSpec((rows, HEAD_DIM),
                                     lambda b, h, i: (b * nr + i, col * N_HEADS + h))
    return pl.pallas_call(
        functools.partial(_retention_kernel, chunk=chunk, rows=rows, scale=HEAD_DIM ** -0.5),
        out_shape=jax.ShapeDtypeStruct((m, w), BF16),
        grid=(batch, N_HEADS, nr),
        in_specs=[
            zspec(Z_RQ), zspec(Z_RK), zspec(Z_RV), zspec(Z_RG),
            pl.BlockSpec((rows, HEAD_DIM), lambda b, h, i: (i, 0)),
            pl.BlockSpec((rows, HEAD_DIM), lambda b, h, i: (i, 0)),
            pl.BlockSpec((1, chunk, chunk), lambda b, h, i: (h, 0, 0)),
            pl.BlockSpec((1, 2 * chunk + 8, LANES), lambda b, h, i: (h, 0, 0)),
            pl.BlockSpec((1, HEAD_DIM), lambda b, h, i: (0, h)),
            pl.BlockSpec((1, HEAD_DIM), lambda b, h, i: (0, h)),
        ],
        out_specs=pl.BlockSpec((rows, HEAD_DIM), lambda b, h, i: (b * nr + i, h)),
        scratch_shapes=[pltpu.VMEM((HEAD_DIM, HEAD_DIM), F32)],
        compiler_params=_cparams("parallel", "parallel", "arbitrary"),
        name="retention",
    )(z, z, z, z, cos2, sin2, dmask, dec, gn_g.reshape(1, w), gn_b.reshape(1, w))


def _merge_kernel(a_ref, b_ref, c_ref, ga_ref, gb_ref, gc_ref, bga_ref, bgb_ref, bgc_ref,
                  wa_ref, wb_ref, wc_ref, o_ref):
    def branch(x_ref, g_ref, bg_ref, w_ref):
        gate = _sigmoid(g_ref[...].astype(F32) + bg_ref[...])
        return gate * _dot(x_ref[...], w_ref[...])

    o_ref[...] = (branch(a_ref, ga_ref, bga_ref, wa_ref)
                  + branch(b_ref, gb_ref, bgb_ref, wb_ref)
                  + branch(c_ref, gc_ref, bgc_ref, wc_ref)).astype(o_ref.dtype)


def _merge(ya, yb, yc, z, b_gate, wa, wb, wc, *, tm, tn):
    m, kdim = ya.shape
    n = wa.shape[1]
    nt = n // tn
    gate_off = Z_GATES * 1024 // tn
    xspec = pl.BlockSpec((tm, kdim), lambda i, j: (i, 0))
    gspec = lambda r: pl.BlockSpec((tm, tn), lambda i, j: (i, gate_off + r * nt + j))
    bspec = lambda r: pl.BlockSpec((1, tn), lambda i, j: (0, r * nt + j))
    wspec = pl.BlockSpec((kdim, tn), lambda i, j: (0, j))
    bg = b_gate.reshape(1, 3 * n)
    return pl.pallas_call(
        _merge_kernel,
        out_shape=jax.ShapeDtypeStruct((m, n), BF16),
        grid=(m // tm, nt),
        in_specs=[xspec, xspec, xspec, gspec(0), gspec(1), gspec(2),
                  bspec(0), bspec(1), bspec(2), wspec, wspec, wspec],
        out_specs=pl.BlockSpec((tm, tn), lambda i, j: (i, j)),
        compiler_params=_cparams("parallel", "arbitrary"),
        name="gated_merge",
    )(ya, yb, yc, z, z, z, bg, bg, bg, wa, wb, wc)


def _resid_proj_kernel(x_ref, a_ref, w_ref, o_ref):
    o_ref[...] = x_ref[...] + _dot(a_ref[...], w_ref[...])


def _resid_proj(x, a, w, *, tm, tn, name):
    m, n = x.shape
    kdim = a.shape[1]
    return pl.pallas_call(
        _resid_proj_kernel,
        out_shape=jax.ShapeDtypeStruct((m, n), F32),
        grid=(m // tm, n // tn),
        in_specs=[pl.BlockSpec((tm, tn), lambda i, j: (i, j)),
                  pl.BlockSpec((tm, kdim), lambda i, j: (i, 0)),
                  pl.BlockSpec((kdim, tn), lambda i, j: (0, j))],
        out_specs=pl.BlockSpec((tm, tn), lambda i, j: (i, j)),
        compiler_params=_cparams("parallel", "arbitrary"),
        name=name,
    )(x, a, w)


def _cross_kernel(x_ref, g_ref, wq_ref, kv_ref, wo_ref, gf_ref, xo_ref, ho_ref, *, scale):
    x = x_ref[...]
    h = _rms(x, g_ref[...]).astype(BF16)
    q = _dot(h, wq_ref[...]).astype(BF16)
    cw = CROSS_HEADS * HEAD_DIM
    outs = []
    for hh in range(CROSS_HEADS):
        lo = hh * HEAD_DIM
        qh = q[:, lo:lo + HEAD_DIM]
        kh = kv_ref[:, lo:lo + HEAD_DIM]
        vh = kv_ref[:, cw + lo:cw + lo + HEAD_DIM]
        s = _dot_nt(qh, kh) * scale
        p = jnp.exp(s - jnp.max(s, axis=-1, keepdims=True))
        den = jnp.sum(p, axis=-1, keepdims=True)
        outs.append(_dot(p.astype(BF16), vh) / den)
    o = jnp.concatenate(outs, axis=-1).astype(BF16)
    x_new = x + _dot(o, wo_ref[...])
    xo_ref[...] = x_new
    ho_ref[...] = _rms(x_new, gf_ref[...]).astype(ho_ref.dtype)


def _cross_attention(x, g_cross, wq, kv, wo, g_ffn, *, seq, mem_len, tm):
    m, d = x.shape
    cw = CROSS_HEADS * HEAD_DIM
    tiles_per_seq = seq // tm
    return pl.pallas_call(
        functools.partial(_cross_kernel, scale=HEAD_DIM ** -0.5),
        out_shape=[jax.ShapeDtypeStruct((m, d), F32), jax.ShapeDtypeStruct((m, d), BF16)],
        grid=(m // tm,),
        in_specs=[
            pl.BlockSpec((tm, d), lambda i: (i, 0)),
            pl.BlockSpec((1, d), lambda i: (0, 0)),
            pl.BlockSpec((d, cw), lambda i: (0, 0)),
            pl.BlockSpec((mem_len, 2 * cw), lambda i: (i // tiles_per_seq, 0)),
            pl.BlockSpec((cw, d), lambda i: (0, 0)),
            pl.BlockSpec((1, d), lambda i: (0, 0)),
        ],
        out_specs=[pl.BlockSpec((tm, d), lambda i: (i, 0)),
                   pl.BlockSpec((tm, d), lambda i: (i, 0))],
        compiler_params=_cparams("parallel"),
        name="cross_attention",
    )(x, g_cross.reshape(1, d), wq, kv, wo, g_ffn.reshape(1, d))


def _ffn_up_kernel(h_ref, hp_ref, wa_ref, wg_ref, cwa_ref, cwg_ref, cba_ref, cbg_ref, o_ref,
                   *, seq_tiles):
    first = (pl.program_id(0) % seq_tiles) == 0
    keep = jnp.where(first, 0.0, 1.0)
    h = h_ref[...]
    hp = hp_ref[...]

    def branch(w_ref, cw_ref, cb_ref, cols):
        w = w_ref[:, cols]
        u = _dot(h, w)
        prev = _dot(hp, w) * keep
        return _causal_conv3(u, prev, cw_ref[:, cols]) + cb_ref[:, cols]

    pieces = [slice(c, c + MXU_N) for c in range(0, o_ref.shape[1], MXU_N)]
    gates = []
    for cols in pieces:
        g = branch(wg_ref, cwg_ref, cbg_ref, cols)
        gates.append(g * _sigmoid(g))
    for cols, gate in zip(pieces, gates):
        a = branch(wa_ref, cwa_ref, cba_ref, cols)
        o_ref[:, cols] = (gate * a).astype(o_ref.dtype)


def _ffn_up(h, w_up, conv_w, conv_b, *, seq, tm, tn):
    m, d = h.shape
    dff = w_up.shape[1] // 2
    nt = dff // tn
    hb = tm // BF16_ROWS
    cb = conv_b.reshape(1, 2 * dff)
    return pl.pallas_call(
        functools.partial(_ffn_up_kernel, seq_tiles=seq // tm),
        out_shape=jax.ShapeDtypeStruct((m, dff), BF16),
        grid=(m // tm, nt),
        in_specs=[
            pl.BlockSpec((tm, d), lambda i, j: (i, 0)),
            pl.BlockSpec((BF16_ROWS, d), lambda i, j: (jnp.maximum(i * hb - 1, 0), 0)),
            pl.BlockSpec((d, tn), lambda i, j: (0, j)),
            pl.BlockSpec((d, tn), lambda i, j: (0, nt + j)),
            pl.BlockSpec((3, tn), lambda i, j: (0, j)),
            pl.BlockSpec((3, tn), lambda i, j: (0, nt + j)),
            pl.BlockSpec((1, tn), lambda i, j: (0, j)),
            pl.BlockSpec((1, tn), lambda i, j: (0, nt + j)),
        ],
        out_specs=pl.BlockSpec((tm, tn), lambda i, j: (i, j)),
        compiler_params=_cparams("parallel", "arbitrary"),
        name="ffn_up_conv_gate",
    )(h, h, w_up, w_up, conv_w, conv_w, cb, cb)


def _final_norm_kernel(x_ref, g_ref, o_ref):
    o_ref[...] = _rms(x_ref[...], g_ref[...])


def _final_norm(x, g, *, tm):
    m, d = x.shape
    return pl.pallas_call(
        _final_norm_kernel,
        out_shape=jax.ShapeDtypeStruct((m, d), F32),
        grid=(m // tm,),
        in_specs=[pl.BlockSpec((tm, d), lambda i: (i, 0)),
                  pl.BlockSpec((1, d), lambda i: (0, 0))],
        out_specs=pl.BlockSpec((tm, d), lambda i: (i, 0)),
        compiler_params=_cparams("parallel"),
        name="final_norm",
    )(x, g.reshape(1, d))


def _split_w_in(w_in_l):
    fw = N_HEADS * HEAD_DIM
    c0 = 3 * fw
    c1 = c0 + N_HEADS
    main = jnp.concatenate([w_in_l[:, :c0], w_in_l[:, c1:]], axis=1).astype(BF16)
    wfl = jnp.pad(w_in_l[:, c0:c1], ((0, 0), (0, LANES - N_HEADS))).astype(BF16)
    return main, wfl


def kernel(x, mem, g_mix, w_in, b_f, b_gate, conv_w, ret_gn_g, ret_gn_b, w_fox_o, w_conv_o,
           w_ret_o, w_out, g_cross, g_mem, w_cq, w_ckv, w_co, g_ffn, w_up, ffn_conv_w,
           ffn_conv_b, w_down, g_final):
    batch, seq, d = x.shape
    mem_len = mem.shape[1]
    depth = w_in.shape[0]
    m = batch * seq
    tm = min(TM, seq)
    tmc = min(TM_CROSS, seq)
    assert seq % tm == 0 and seq % TQ == 0 and seq % RET_ROWS == 0 and seq % CUM_CHUNK == 0
    assert w_in.shape[2] - N_HEADS == Z_WIDTH

    xf = x.reshape(m, d)
    memf = mem.reshape(batch * mem_len, d)
    for l in range(depth):
        w_main, w_fl = _split_w_in(w_in[l])
        z, fl = _norm_proj(xf, g_mix[l], w_main, w_fl, tm=tm, tn=TN_IN)
        bf_pad = jnp.pad(b_f[l], (0, LANES - N_HEADS)).reshape(1, LANES)
        ccol, crow = _forget_cumsum(fl, bf_pad, batch=batch, seq=seq)
        ya = _fox_attention(z, ccol, crow, batch=batch, seq=seq)
        yb = _conv_branch(z, conv_w[l], seq=seq, tm=tm)
        yc = _retention(z, ret_gn_g[l], ret_gn_b[l], batch=batch, seq=seq)
        mg = _merge(ya, yb, yc, z, b_gate[l], w_fox_o[l].astype(BF16),
                    w_conv_o[l].astype(BF16), w_ret_o[l].astype(BF16), tm=tm, tn=TN)
        x1 = _resid_proj(xf, mg, w_out[l].astype(BF16), tm=tm, tn=TN, name="out_proj")
        kv = _norm_proj(memf, g_mem[l], w_ckv[l].astype(BF16), tm=mem_len, tn=TN)
        x2, h3 = _cross_attention(x1, g_cross[l], w_cq[l].astype(BF16), kv,
                                  w_co[l].astype(BF16), g_ffn[l], seq=seq, mem_len=mem_len,
                                  tm=tmc)
        act = _ffn_up(h3, w_up[l].astype(BF16), ffn_conv_w[l], ffn_conv_b[l],
                      seq=seq, tm=tm, tn=TN)
        xf = _resid_proj(x2, act, w_down[l].astype(BF16), tm=tm, tn=TN, name="down_proj")
    return _final_norm(xf, g_final, tm=tm).reshape(batch, seq, d)
```
